```python
import math
import jax, jax.numpy as jnp
from jax import lax
import numpy as np

D_MODEL = 1024
BATCH = 4
SEQ = 8192
DEPTH = 1

D_RNN = D_MODEL
N_RNN_BLOCKS = 16
RNN_BLOCK = D_RNN // N_RNN_BLOCKS
CONV_WIDTH = 4
LRU_C = 8.0
N_DIFF_HEADS = 8
DIFF_HEAD_DIM = 64
D_QK = N_DIFF_HEADS * 2 * DIFF_HEAD_DIM
D_ATTN = N_DIFF_HEADS * 2 * DIFF_HEAD_DIM
Q_BLOCK = 128
N_BUCKETS = 32
MAX_DISTANCE = 128
LN_EPS = 1e-5
DEEPNORM_ALPHA = (2.0 * DEPTH) ** 0.25
DEEPNORM_BETA = (8.0 * DEPTH) ** -0.25
IN_WIDTHS = (D_RNN, D_RNN, D_QK, D_QK, D_ATTN, D_ATTN, 2 * D_MODEL)
IN_WIDTH = sum(IN_WIDTHS)
SPLIT_POINTS = tuple(int(v) for v in np.cumsum(IN_WIDTHS)[:-1])

kernel_name = "hybrid_rglru_diffattn_gated_merge"


def layer_norm(z, g, b):
    z32 = z.astype(jnp.float32)
    mu = jnp.mean(z32, axis=-1, keepdims=True)
    var = jnp.mean(jnp.square(z32 - mu), axis=-1, keepdims=True)
    return ((z32 - mu) * lax.rsqrt(var + LN_EPS) * g.astype(jnp.float32) + b.astype(jnp.float32)).astype(z.dtype)


def rms_norm(z, g):
    z32 = z.astype(jnp.float32)
    return z32 * lax.rsqrt(jnp.mean(jnp.square(z32), axis=-1, keepdims=True) + LN_EPS) * g.astype(jnp.float32)


def causal_depthwise_conv(x, w, b):
    y = lax.conv_general_dilated(
        x, w[:, None, :].astype(x.dtype), window_strides=(1,),
        padding=[(CONV_WIDTH - 1, 0)],
        dimension_numbers=("NWC", "WIO", "NWC"),
        feature_group_count=x.shape[-1])
    return y + b.astype(x.dtype)


def rg_lru(xr, w_a, b_a, w_x, b_x, lam_param):
    B, S, _ = xr.shape
    x32 = xr.astype(jnp.float32)
    xb = x32.reshape(B, S, N_RNN_BLOCKS, RNN_BLOCK)
    r = jax.nn.sigmoid(jnp.einsum("bshi,hij->bshj", xb, w_a.astype(jnp.float32)).reshape(B, S, D_RNN) + b_a.astype(jnp.float32))
    i = jax.nn.sigmoid(jnp.einsum("bshi,hij->bshj", xb, w_x.astype(jnp.float32)).reshape(B, S, D_RNN) + b_x.astype(jnp.float32))
    log_a = -LRU_C * r * jax.nn.softplus(-lam_param.astype(jnp.float32))
    a = jnp.exp(log_a)
    b_in = jnp.sqrt(-jnp.expm1(2.0 * log_a)) * (i * x32)

    def combine(left, right):
        a1, b1 = left
        a2, b2 = right
        return a1 * a2, a2 * b1 + b2

    _, h = lax.associative_scan(combine, (a, b_in), axis=1)
    return h


def t5_causal_bucket(q_pos, k_pos):
    n = jnp.maximum(q_pos[:, None] - k_pos[None, :], 0)
    max_exact = N_BUCKETS // 2
    nf = jnp.maximum(n, 1).astype(jnp.float32)
    large = max_exact + (jnp.log(nf / max_exact) / math.log(MAX_DISTANCE / max_exact)
                         * (N_BUCKETS - max_exact)).astype(jnp.int32)
    large = jnp.minimum(large, N_BUCKETS - 1)
    return jnp.where(n < max_exact, n, large)


def diff_attention(q, k, v, rel_bias, lam):
    B, S = q.shape[0], q.shape[1]
    nb = S // Q_BLOCK
    scale = DIFF_HEAD_DIM ** -0.5
    qb = (q.astype(jnp.float32) * scale).reshape(
        B, nb, Q_BLOCK, N_DIFF_HEADS, 2, DIFF_HEAD_DIM).transpose(1, 0, 2, 3, 4, 5)
    k32 = k.astype(jnp.float32)
    v32 = v.astype(jnp.float32)
    bias_table = rel_bias.astype(jnp.float32)
    k_pos = jnp.arange(S)

    def block(args):
        qi, bi = args
        q_pos = bi * Q_BLOCK + jnp.arange(Q_BLOCK)
        bias = bias_table[t5_causal_bucket(q_pos, k_pos)].transpose(2, 0, 1)
        logits = jnp.einsum("bqhcd,bkhcd->cbhqk", qi, k32) + bias[None, None]
        causal = q_pos[:, None] >= k_pos[None, :]
        logits = jnp.where(causal, logits, -jnp.inf)
        p = jax.nn.softmax(logits, axis=-1)
        w = p[0] - lam * p[1]
        return jnp.einsum("bhqk,bkhe->bqhe", w, v32)

    out = lax.map(block, (qb, jnp.arange(nb)))
    return out.transpose(1, 0, 2, 3, 4).reshape(B, S, N_DIFF_HEADS, 2 * DIFF_HEAD_DIM)


def setup_inputs(seed: int = 0) -> dict:
    key = jax.random.key(seed)
    ks = jax.random.split(key, 24)
    f32 = jnp.float32
    nrm = lambda k, shape, s: jax.random.normal(k, shape, f32) * s
    u = jax.random.uniform(ks[10], (DEPTH, D_RNN), f32, minval=0.9, maxval=0.999)
    a0 = u ** (1.0 / LRU_C)
    return {
        "x": nrm(ks[0], (BATCH, SEQ, D_MODEL), 1.0),
        "c": nrm(ks[1], (BATCH, D_MODEL), 1.0),
        "w_ada": nrm(ks[2], (DEPTH, D_MODEL, 3 * D_MODEL), 0.5 * D_MODEL ** -0.5),
        "b_ada": nrm(ks[3], (DEPTH, 3 * D_MODEL), 0.01),
        "w_in": nrm(ks[4], (DEPTH, D_MODEL, IN_WIDTH), D_MODEL ** -0.5),
        "conv_w": nrm(ks[5], (DEPTH, CONV_WIDTH, D_RNN), CONV_WIDTH ** -0.5),
        "conv_b": nrm(ks[6], (DEPTH, D_RNN), 0.01),
        "w_rg_a": nrm(ks[7], (DEPTH, N_RNN_BLOCKS, RNN_BLOCK, RNN_BLOCK), RNN_BLOCK ** -0.5),
        "b_rg_a": nrm(ks[8], (DEPTH, D_RNN), 0.01),
        "w_rg_x": nrm(ks[9], (DEPTH, N_RNN_BLOCKS, RNN_BLOCK, RNN_BLOCK), RNN_BLOCK ** -0.5),
        "b_rg_x": nrm(ks[11], (DEPTH, D_RNN), 0.01),
        "lru_lambda": jnp.log(a0) - jnp.log1p(-a0),
        "lambda_q1": nrm(ks[12], (DEPTH, DIFF_HEAD_DIM), 0.1),
        "lambda_k1": nrm(ks[13], (DEPTH, DIFF_HEAD_DIM), 0.1),
        "lambda_q2": nrm(ks[14], (DEPTH, DIFF_HEAD_DIM), 0.1),
        "lambda_k2": nrm(ks[15], (DEPTH, DIFF_HEAD_DIM), 0.1),
        "subln_w": 1.0 + nrm(ks[16], (DEPTH, 2 * DIFF_HEAD_DIM), 0.02),
        "rel_bias": nrm(ks[17], (N_BUCKETS, N_DIFF_HEADS), 0.2),
        "w_rnn_out": nrm(ks[18], (DEPTH, D_RNN, D_MODEL), DEEPNORM_BETA * D_RNN ** -0.5),
        "w_attn_out": nrm(ks[19], (DEPTH, D_ATTN, D_MODEL), DEEPNORM_BETA * D_ATTN ** -0.5),
        "w_out": nrm(ks[20], (DEPTH, D_MODEL, D_MODEL), DEEPNORM_BETA * D_MODEL ** -0.5),
        "ln_g": 1.0 + nrm(ks[21], (DEPTH, D_MODEL), 0.02),
        "ln_b": nrm(ks[22], (DEPTH, D_MODEL), 0.02),
    }


def reference(x, c, w_ada, b_ada, w_in, conv_w, conv_b, w_rg_a, b_rg_a, w_rg_x, b_rg_x,
              lru_lambda, lambda_q1, lambda_k1, lambda_q2, lambda_k2, subln_w, rel_bias,
              w_rnn_out, w_attn_out, w_out, ln_g, ln_b):
    B, S, _ = x.shape
    dtype = x.dtype
    h = x
    for layer in range(DEPTH):
        lambda_init = 0.8 - 0.6 * math.exp(-0.3 * layer)
        mod = jnp.einsum("bd,de->be", jax.nn.silu(c), w_ada[layer]) + b_ada[layer]
        shift, scale, gate = jnp.split(mod, 3, axis=-1)
        u = h * (1.0 + scale[:, None, :]) + shift[:, None, :]
        proj = jnp.einsum("bsd,de->bse", u, w_in[layer])
        rx, rg, q, k, v, ag, mg = jnp.split(proj, SPLIT_POINTS, axis=-1)

        rx = causal_depthwise_conv(rx, conv_w[layer], conv_b[layer])
        hr = rg_lru(rx, w_rg_a[layer], b_rg_a[layer], w_rg_x[layer], b_rg_x[layer], lru_lambda[layer])
        y_rnn = (hr * jax.nn.silu(rg.astype(jnp.float32))).astype(dtype)
        y_rnn = jnp.einsum("bse,ed->bsd", y_rnn, w_rnn_out[layer])

        q = q.reshape(B, S, N_DIFF_HEADS, 2, DIFF_HEAD_DIM)
        k = k.reshape(B, S, N_DIFF_HEADS, 2, DIFF_HEAD_DIM)
        v = v.reshape(B, S, N_DIFF_HEADS, 2 * DIFF_HEAD_DIM)
        lam = (jnp.exp(jnp.sum(lambda_q1[layer].astype(jnp.float32) * lambda_k1[layer].astype(jnp.float32)))
               - jnp.exp(jnp.sum(lambda_q2[layer].astype(jnp.float32) * lambda_k2[layer].astype(jnp.float32)))
               + lambda_init)
        o = diff_attention(q, k, v, rel_bias, lam)
        o = rms_norm(o, subln_w[layer]) * (1.0 - lambda_init)
        y_attn = (o.reshape(B, S, D_ATTN) * jax.nn.silu(ag.astype(jnp.float32))).astype(dtype)
        y_attn = jnp.einsum("bse,ed->bsd", y_attn, w_attn_out[layer])

        g_rnn, g_attn = jnp.split(jax.nn.sigmoid(mg), 2, axis=-1)
        merged = g_rnn * y_rnn + g_attn * y_attn
        out = jnp.einsum("bsd,de->bse", merged, w_out[layer]) * gate[:, None, :]

        h = layer_norm(DEEPNORM_ALPHA * h + out, ln_g[layer], ln_b[layer])
    return h.astype(dtype)
```

```python
import functools
import math

import jax
import jax.numpy as jnp
from jax import lax
from jax.experimental import pallas as pl
from jax.experimental.pallas import tpu as pltpu

F32 = jnp.float32
BF16 = jnp.bfloat16

D_MODEL = 1024
N_RNN_BLOCKS = 16
RNN_BLOCK = D_MODEL // N_RNN_BLOCKS
CONV_WIDTH = 4
LRU_C = 8.0
N_HEADS = 8
HEAD_DIM = 64
HEAD_W = 2 * HEAD_DIM
N_BUCKETS = 32
MAX_EXACT = N_BUCKETS // 2
MAX_DISTANCE = 128
LN_EPS = 1e-5
DEPTH = 1
DEEPNORM_ALPHA = (2.0 * DEPTH) ** 0.25
LAMBDA_INIT = 0.8 - 0.6 * math.exp(-0.3 * 0)

MXU_TILE = 256
GATE_GROUPS = D_MODEL // MXU_TILE

COL_RG, COL_Q, COL_K, COL_AG, COL_MG = 0, 1, 2, 3, 4
NAT_WIDTH = 6 * D_MODEL

PROJ_TS = 1024
PROJ_TN = 1024
RNN_TS = 512
ATT_T = 256
FIN_TS = 512

VMEM_LIMIT = 48 * 1024 * 1024


def _cparams(n_axes):
    return pltpu.CompilerParams(
        dimension_semantics=("arbitrary",) * n_axes,
        vmem_limit_bytes=VMEM_LIMIT,
    )


def _sigmoid(z):
    return 1.0 / (1.0 + jnp.exp(-z))


def _silu(z):
    return z * _sigmoid(z)


def _mod_kernel(c_ref, w_ref, b_ref, o_ref):
    c = c_ref[...]
    o_ref[...] = jnp.dot(_silu(c), w_ref[...], preferred_element_type=F32,
                         precision=lax.Precision.HIGHEST) + b_ref[...]


def _mod_call(c, w_ada, b_ada):
    bsz = c.shape[0]
    n = w_ada.shape[1]
    tn = 1024
    return pl.pallas_call(
        _mod_kernel,
        grid=(n // tn,),
        in_specs=[
            pl.BlockSpec((bsz, D_MODEL), lambda j: (0, 0)),
            pl.BlockSpec((D_MODEL, tn), lambda j: (0, j)),
            pl.BlockSpec((1, tn), lambda j: (0, j)),
        ],
        out_specs=pl.BlockSpec((bsz, tn), lambda j: (0, j)),
        out_shape=jax.ShapeDtypeStruct((bsz, n), F32),
        compiler_params=_cparams(1),
        name="mod",
    )(c, w_ada, b_ada.reshape(1, n))


def _modulated(x_ref, mod_ref):
    shift = mod_ref[:, 0:D_MODEL]
    scale = mod_ref[:, D_MODEL:2 * D_MODEL]
    return (x_ref[...] * (1.0 + scale) + shift).astype(BF16)


def _proj_nat_kernel(x_ref, mod_ref, w_ref, cs_ref, o_ref):
    u = _modulated(x_ref, mod_ref)
    acc = jnp.dot(u, w_ref[...], preferred_element_type=F32)
    o_ref[...] = (acc * cs_ref[...]).astype(o_ref.dtype)


def _proj_nat_call(x, mod3, w, colscale, out_dtype):
    bsz, seq, _ = x.shape
    n = w.shape[1]
    return pl.pallas_call(
        _proj_nat_kernel,
        grid=(bsz, seq // PROJ_TS, n // PROJ_TN),
        in_specs=[
            pl.BlockSpec((None, PROJ_TS, D_MODEL), lambda b, s, j: (b, s, 0)),
            pl.BlockSpec((None, 1, 3 * D_MODEL), lambda b, s, j: (b, 0, 0)),
            pl.BlockSpec((D_MODEL, PROJ_TN), lambda b, s, j: (0, j)),
            pl.BlockSpec((1, PROJ_TN), lambda b, s, j: (0, j)),
        ],
        out_specs=pl.BlockSpec((None, PROJ_TS, PROJ_TN), lambda b, s, j: (b, s, j)),
        out_shape=jax.ShapeDtypeStruct((bsz, seq, n), out_dtype),
        compiler_params=_cparams(3),
        name="proj_nat_" + jnp.dtype(out_dtype).name,
    )(x, mod3, w, colscale)


def _proj_t_kernel(x_ref, mod_ref, wt_ref, o_ref):
    u = _modulated(x_ref, mod_ref)
    acc = lax.dot_general(wt_ref[...], u, (((1,), (1,)), ((), ())),
                          preferred_element_type=F32)
    for jj in range(PROJ_TS // ATT_T):
        o_ref[jj] = acc[:, jj * ATT_T:(jj + 1) * ATT_T].astype(o_ref.dtype)


def _proj_t_call(x, mod3, wt):
    bsz, seq, _ = x.shape
    n = wt.shape[0]
    per = PROJ_TS // ATT_T
    return pl.pallas_call(
        _proj_t_kernel,
        grid=(bsz, seq // PROJ_TS),
        in_specs=[
            pl.BlockSpec((None, PROJ_TS, D_MODEL), lambda b, s: (b, s, 0)),
            pl.BlockSpec((None, 1, 3 * D_MODEL), lambda b, s: (b, 0, 0)),
            pl.BlockSpec((n, D_MODEL), lambda b, s: (0, 0)),
        ],
        out_specs=pl.BlockSpec((None, per, n, ATT_T), lambda b, s: (b, s, 0, 0)),
        out_shape=jax.ShapeDtypeStruct((bsz, seq // ATT_T, n, ATT_T), BF16),
        compiler_params=_cparams(2),
        name="proj_t",
    )(x, mod3, wt)


def _neg_expm1(z):
    e = jnp.exp(z)
    em1 = e - 1.0
    corrected = em1 * z / jnp.log(e)
    return -jnp.where(e == 1.0, z, jnp.where(em1 == -1.0, -1.0, corrected))


def _rnn_kernel(rx_ref, rg_ref, cw_ref, cb_ref, wa_ref, ba_ref, wx_ref, bx_ref, lam_ref,
                wo_ref, o_ref, xbuf, hcar, acum, bcum):
    ts = RNN_TS

    @pl.when(pl.program_id(1) == 0)
    def _():
        xbuf[0:8, :] = jnp.zeros((8, D_MODEL), F32)
        hcar[...] = jnp.zeros((1, D_MODEL), F32)

    x = rx_ref[...]
    xbuf[8:8 + ts, :] = x
    conv = (cw_ref[3:4, :] * x
            + cw_ref[2:3, :] * xbuf[7:7 + ts, :]
            + cw_ref[1:2, :] * xbuf[6:6 + ts, :]
            + cw_ref[0:1, :] * xbuf[5:5 + ts, :]
            + cb_ref[...])
    xbuf[0:8, :] = x[ts - 8:ts, :]

    cb16 = conv.astype(BF16)
    r_parts, i_parts = [], []
    for g in range(GATE_GROUPS):
        xg = cb16[:, g * MXU_TILE:(g + 1) * MXU_TILE]
        r_parts.append(jnp.dot(xg, wa_ref[g], preferred_element_type=F32))
        i_parts.append(jnp.dot(xg, wx_ref[g], preferred_element_type=F32))
    r = _sigmoid(jnp.concatenate(r_parts, axis=1) + ba_ref[...])
    ig = _sigmoid(jnp.concatenate(i_parts, axis=1) + bx_ref[...])

    nl = -lam_ref[...]
    softplus = jnp.maximum(nl, 0.0) + jnp.log(1.0 + jnp.exp(-jnp.abs(nl)))
    log_a = (-LRU_C) * r * softplus
    a = jnp.exp(log_a)
    b = jnp.sqrt(_neg_expm1(2.0 * log_a)) * (ig * conv)

    row = lax.broadcasted_iota(jnp.int32, (ts, D_MODEL), 0) % 8
    for d in (1, 2, 4):
        keep = row >= d
        a_sh = jnp.where(keep, pltpu.roll(a, d, 0), 1.0)
        b_sh = jnp.where(keep, pltpu.roll(b, d, 0), 0.0)
        b = b + a * b_sh
        a = a * a_sh
    acum[...] = a
    bcum[...] = b

    def group(g, carry):
        r0 = pl.multiple_of(g * 8, 8)
        hg = acum[pl.ds(r0, 8), :] * carry + bcum[pl.ds(r0, 8), :]
        bcum[pl.ds(r0, 8), :] = hg
        return hg[7:8, :]

    hcar[...] = lax.fori_loop(0, ts // 8, group, hcar[...])

    y = (bcum[...] * _silu(rg_ref[...].astype(F32))).astype(BF16)
    o_ref[...] = jnp.dot(y, wo_ref[...], preferred_element_type=F32).astype(o_ref.dtype)


def _rnn_call(rx, nat, conv_w, conv_b, wa, ba, wx, bx, lam, w_out):
    bsz, seq, _ = rx.shape
    row = lambda v: v.reshape(1, D_MODEL)
    const2 = lambda b, t: (0, 0)
    return pl.pallas_call(
        _rnn_kernel,
        grid=(bsz, seq // RNN_TS),
        in_specs=[
            pl.BlockSpec((None, RNN_TS, D_MODEL), lambda b, t: (b, t, 0)),
            pl.BlockSpec((None, RNN_TS, D_MODEL), lambda b, t: (b, t, COL_RG)),
            pl.BlockSpec((CONV_WIDTH, D_MODEL), const2),
            pl.BlockSpec((1, D_MODEL), const2),
            pl.BlockSpec((GATE_GROUPS, MXU_TILE, MXU_TILE), lambda b, t: (0, 0, 0)),
            pl.BlockSpec((1, D_MODEL), const2),
            pl.BlockSpec((GATE_GROUPS, MXU_TILE, MXU_TILE), lambda b, t: (0, 0, 0)),
            pl.BlockSpec((1, D_MODEL), const2),
            pl.BlockSpec((1, D_MODEL), const2),
            pl.BlockSpec((D_MODEL, D_MODEL), const2),
        ],
        out_specs=pl.BlockSpec((None, RNN_TS, D_MODEL), lambda b, t: (b, t, 0)),
        out_shape=jax.ShapeDtypeStruct((bsz, seq, D_MODEL), F32),
        scratch_shapes=[
            pltpu.VMEM((RNN_TS + 8, D_MODEL), F32),
            pltpu.VMEM((1, D_MODEL), F32),
            pltpu.VMEM((RNN_TS, D_MODEL), F32),
            pltpu.VMEM((RNN_TS, D_MODEL), F32),
        ],
        compiler_params=_cparams(2),
        name="rnn",
    )(rx, nat, conv_w, row(conv_b), wa, row(ba), wx, row(bx), row(lam), w_out)


def _rel_bias_tile(rb_ref, head, dist):
    n = jnp.maximum(dist, 0)
    nf = jnp.maximum(n, 1).astype(F32)
    large = MAX_EXACT + (jnp.log(nf / MAX_EXACT) / math.log(MAX_DISTANCE / MAX_EXACT)
                         * (N_BUCKETS - MAX_EXACT)).astype(jnp.int32)
    large = jnp.minimum(large, N_BUCKETS - 1)
    bucket = jnp.where(n < MAX_EXACT, n, large)
    far = rb_ref[N_BUCKETS - 1, head]
    out = jnp.zeros(dist.shape, F32)
    for bkt in range(N_BUCKETS - 1):
        out = jnp.where(bucket == bkt, rb_ref[bkt, head] - far, out)
    return out


def _attn_kernel(rb_ref, lq1_ref, lk1_ref, lq2_ref, lk2_ref, sw_ref, q_ref, k_ref, vt_ref, ag_ref,
                 o_ref, bdiag, bsub, m_ref, l_ref, acc_ref):
    t = ATT_T
    head = pl.program_id(1)
    qi = pl.program_id(2)

    @pl.when(qi == 0)
    def _():
        kk = lax.broadcasted_iota(jnp.int32, (t, t), 0)
        qq = lax.broadcasted_iota(jnp.int32, (t, t), 1)
        dist = qq - kk
        bdiag[...] = jnp.where(dist >= 0, _rel_bias_tile(rb_ref, head, dist), -jnp.inf)
        bsub[...] = _rel_bias_tile(rb_ref, head, dist + t)

    q = q_ref[...]
    lane = lax.broadcasted_iota(jnp.int32, (t, HEAD_W), 1)
    zero = jnp.zeros_like(q)
    wq = jnp.concatenate([jnp.where(lane < HEAD_DIM, q, zero),
                          jnp.where(lane >= HEAD_DIM, q, zero)], axis=0)

    m_ref[...] = jnp.full((1, 2 * t), -jnp.inf, F32)
    l_ref[...] = jnp.zeros((1, 2 * t), F32)
    acc_ref[...] = jnp.zeros((HEAD_W, 2 * t), F32)

    def step(j, bias):
        k0 = pl.multiple_of(j * t, t)
        kj = k_ref[pl.ds(k0, t), :]
        s = lax.dot_general(kj, wq, (((1,), (1,)), ((), ())), preferred_element_type=F32)
        if bias is not None:
            s = s + jnp.concatenate([bias, bias], axis=1)
        m_old = m_ref[...]
        m_new = jnp.maximum(m_old, jnp.max(s, axis=0, keepdims=True))
        p = jnp.exp(s - m_new)
        alpha = jnp.exp(m_old - m_new)
        l_ref[...] = alpha * l_ref[...] + jnp.sum(p, axis=0, keepdims=True)
        pv = jnp.dot(vt_ref[j], p.astype(BF16), preferred_element_type=F32)
        acc_ref[...] = alpha * acc_ref[...] + pv
        m_ref[...] = m_new

    def far_step(j, carry):
        step(j, None)
        return carry

    lax.fori_loop(0, jnp.maximum(qi - 1, 0), far_step, 0)

    @pl.when(qi >= 1)
    def _():
        step(qi - 1, bsub[...])

    step(qi, bdiag[...])

    inv_l = 1.0 / l_ref[...]
    on = acc_ref[...] * inv_l
    lam = (jnp.exp(jnp.sum(lq1_ref[...] * lk1_ref[...], axis=1, keepdims=True))
           - jnp.exp(jnp.sum(lq2_ref[...] * lk2_ref[...], axis=1, keepdims=True))
           + LAMBDA_INIT)
    o = on[:, :t] - lam * on[:, t:]
    ms = jnp.mean(o * o, axis=0, keepdims=True)
    o = (o * lax.rsqrt(ms + LN_EPS)).T
    o = o * sw_ref[...] * (1.0 - LAMBDA_INIT)
    o_ref[...] = (o * _silu(ag_ref[...].astype(F32))).astype(o_ref.dtype)


def _attn_call(nat, vt, rel_bias, lq1, lk1, lq2, lk2, subln_w):
    bsz, seq, _ = nat.shape
    nt = seq // ATT_T
    hb = D_MODEL // HEAD_W
    vec = lambda v: v.reshape(1, -1)
    const2 = lambda b, h, i: (0, 0)
    return pl.pallas_call(
        _attn_kernel,
        grid=(bsz, N_HEADS, nt),
        in_specs=[
            pl.BlockSpec(memory_space=pltpu.SMEM),
            pl.BlockSpec((1, HEAD_DIM), const2),
            pl.BlockSpec((1, HEAD_DIM), const2),
            pl.BlockSpec((1, HEAD_DIM), const2),
            pl.BlockSpec((1, HEAD_DIM), const2),
            pl.BlockSpec((1, HEAD_W), const2),
            pl.BlockSpec((None, ATT_T, HEAD_W), lambda b, h, i: (b, i, COL_Q * hb + h)),
            pl.BlockSpec((None, seq, HEAD_W), lambda b, h, i: (b, 0, COL_K * hb + h)),
            pl.BlockSpec((None, nt, HEAD_W, ATT_T), lambda b, h, i: (b, 0, h, 0)),
            pl.BlockSpec((None, ATT_T, HEAD_W), lambda b, h, i: (b, i, COL_AG * hb + h)),
        ],
        out_specs=pl.BlockSpec((None, ATT_T, HEAD_W), lambda b, h, i: (b, i, h)),
        out_shape=jax.ShapeDtypeStruct((bsz, seq, D_MODEL), BF16),
        scratch_shapes=[
            pltpu.VMEM((ATT_T, ATT_T), F32),
            pltpu.VMEM((ATT_T, ATT_T), F32),
            pltpu.VMEM((1, 2 * ATT_T), F32),
            pltpu.VMEM((1, 2 * ATT_T), F32),
            pltpu.VMEM((HEAD_W, 2 * ATT_T), F32),
        ],
        compiler_params=_cparams(3),
        name="attn",
    )(rel_bias, vec(lq1), vec(lk1), vec(lq2), vec(lk2), vec(subln_w), nat, nat, vt, nat)


def _final_kernel(x_ref, mod_ref, yr_ref, ya_ref, mg_ref, wa_ref, wo_ref, g_ref, b_ref, o_ref):
    gate = mod_ref[:, 2 * D_MODEL:3 * D_MODEL]
    y_attn = jnp.dot(ya_ref[...], wa_ref[...], preferred_element_type=F32)
    mg = _sigmoid(mg_ref[...].astype(F32))
    merged = mg[:, :D_MODEL] * yr_ref[...] + mg[:, D_MODEL:] * y_attn
    out = jnp.dot(merged.astype(BF16), wo_ref[...], preferred_element_type=F32) * gate
    z = DEEPNORM_ALPHA * x_ref[...] + out
    mu = jnp.mean(z, axis=-1, keepdims=True)
    zc = z - mu
    var = jnp.mean(zc * zc, axis=-1, keepdims=True)
    o_ref[...] = zc * lax.rsqrt(var + LN_EPS) * g_ref[...] + b_ref[...]


def _final_call(x, mod3, y_rnn, y_attn_pre, nat, w_attn_out, w_out, ln_g, ln_b):
    bsz, seq, _ = x.shape
    row = lambda v: v.reshape(1, D_MODEL)
    const2 = lambda b, s: (0, 0)
    tile = lambda: pl.BlockSpec((None, FIN_TS, D_MODEL), lambda b, s: (b, s, 0))
    return pl.pallas_call(
        _final_kernel,
        grid=(bsz, seq // FIN_TS),
        in_specs=[
            tile(),
            pl.BlockSpec((None, 1, 3 * D_MODEL), lambda b, s: (b, 0, 0)),
            tile(),
            tile(),
            pl.BlockSpec((None, FIN_TS, 2 * D_MODEL), lambda b, s: (b, s, COL_MG // 2)),
            pl.BlockSpec((D_MODEL, D_MODEL), const2),
            pl.BlockSpec((D_MODEL, D_MODEL), const2),
            pl.BlockSpec((1, D_MODEL), const2),
            pl.BlockSpec((1, D_MODEL), const2),
        ],
        out_specs=tile(),
        out_shape=jax.ShapeDtypeStruct((bsz, seq, D_MODEL), F32),
        compiler_params=_cparams(2),
        name="final",
    )(x, mod3, y_rnn, y_attn_pre, nat, w_attn_out, w_out, row(ln_g), row(ln_b))


def _block_diag_groups(w):
    per = MXU_TILE // RNN_BLOCK
    w = w.reshape(GATE_GROUPS, per, RNN_BLOCK, RNN_BLOCK)
    eye = jnp.eye(per, dtype=w.dtype)
    return jnp.einsum("gaij,ab->gaibj", w, eye).reshape(GATE_GROUPS, MXU_TILE, MXU_TILE)


def kernel(x, c, w_ada, b_ada, w_in, conv_w, conv_b, w_rg_a, b_rg_a, w_rg_x, b_rg_x, lru_lambda,
           lambda_q1, lambda_k1, lambda_q2, lambda_k2, subln_w, rel_bias, w_rnn_out, w_attn_out,
           w_out, ln_g, ln_b):
    layer = 0
    bsz = x.shape[0]
    d = D_MODEL

    mod3 = _mod_call(c, w_ada[layer], b_ada[layer]).reshape(bsz, 1, 3 * d)

    w_in16 = w_in[layer].astype(BF16)
    w_rx = w_in16[:, 0:d]
    w_nat = jnp.concatenate([w_in16[:, d:4 * d], w_in16[:, 5 * d:8 * d]], axis=1)
    w_vt = w_in16[:, 4 * d:5 * d].T
    ones = jnp.ones((1, d), F32)
    colscale = jnp.concatenate(
        [ones, jnp.full((1, d), HEAD_DIM ** -0.5, F32), ones, ones, ones, ones], axis=1)

    rx = _proj_nat_call(x, mod3, w_rx, ones, F32)
    nat = _proj_nat_call(x, mod3, w_nat, colscale, BF16)
    vt = _proj_t_call(x, mod3, w_vt)

    y_rnn = _rnn_call(rx, nat, conv_w[layer], conv_b[layer],
                      _block_diag_groups(w_rg_a[layer]).astype(BF16), b_rg_a[layer],
                      _block_diag_groups(w_rg_x[layer]).astype(BF16), b_rg_x[layer],
                      lru_lambda[layer], w_rnn_out[layer].astype(BF16))

    y_attn_pre = _attn_call(nat, vt, rel_bias, lambda_q1[layer], lambda_k1[layer],
                            lambda_q2[layer], lambda_k2[layer], subln_w[layer])

    return _final_call(x, mod3, y_rnn, y_attn_pre, nat, w_attn_out[layer].astype(BF16),
                       w_out[layer].astype(BF16), ln_g[layer], ln_b[layer])
```

```python
import functools
import math

import jax
import jax.numpy as jnp
from jax import lax
from jax.experimental import pallas as pl
from jax.experimental.pallas import tpu as pltpu

F32 = jnp.float32
BF16 = jnp.bfloat16

D_MODEL = 1024
N_RNN_BLOCKS = 16
RNN_BLOCK = D_MODEL // N_RNN_BLOCKS
CONV_WIDTH = 4
LRU_C = 8.0
N_HEADS = 8
HEAD_DIM = 64
HEAD_W = 2 * HEAD_DIM
N_BUCKETS = 32
MAX_EXACT = N_BUCKETS // 2
MAX_DISTANCE = 128
LN_EPS = 1e-5
DEPTH = 1
DEEPNORM_ALPHA = (2.0 * DEPTH) ** 0.25
LAMBDA_INIT = 0.8 - 0.6 * math.exp(-0.3 * 0)

MXU_TILE = 256
GATE_GROUPS = D_MODEL // MXU_TILE

COL_RG, COL_Q, COL_K, COL_AG, COL_MG = 0, 1, 2, 3, 4
NAT_WIDTH = 6 * D_MODEL

PROJ_TS = 1024
PROJ_TN = 1024
RNN_TS = 512
ATT_T = 256
ATT_FAR_UNROLL = 15
ATT_NEAR_UNROLL = 9
V_ROWS = HEAD_W + 16
LOG2E = 1.4426950408889634
FIN_TS = 512

VMEM_LIMIT = 48 * 1024 * 1024
ATT_VMEM_LIMIT = 56 * 1024 * 1024


def _cparams(n_axes):
    return pltpu.CompilerParams(
        dimension_semantics=("arbitrary",) * n_axes,
        vmem_limit_bytes=VMEM_LIMIT,
    )


def _sigmoid(z):
    return 1.0 / (1.0 + jnp.exp(-z))


def _silu(z):
    return z * _sigmoid(z)


def _mod_kernel(c_ref, w_ref, b_ref, o_ref):
    c = c_ref[...]
    o_ref[...] = jnp.dot(_silu(c), w_ref[...], preferred_element_type=F32,
                         precision=lax.Precision.HIGHEST) + b_ref[...]


def _mod_call(c, w_ada, b_ada):
    bsz = c.shape[0]
    n = w_ada.shape[1]
    tn = 1024
    return pl.pallas_call(
        _mod_kernel,
        grid=(n // tn,),
        in_specs=[
            pl.BlockSpec((bsz, D_MODEL), lambda j: (0, 0)),
            pl.BlockSpec((D_MODEL, tn), lambda j: (0, j)),
            pl.BlockSpec((1, tn), lambda j: (0, j)),
        ],
        out_specs=pl.BlockSpec((bsz, tn), lambda j: (0, j)),
        out_shape=jax.ShapeDtypeStruct((bsz, n), F32),
        compiler_params=_cparams(1),
        name="mod",
    )(c, w_ada, b_ada.reshape(1, n))


def _modulated(x_ref, mod_ref):
    shift = mod_ref[:, 0:D_MODEL]
    scale = mod_ref[:, D_MODEL:2 * D_MODEL]
    return (x_ref[...] * (1.0 + scale) + shift).astype(BF16)


def _proj_nat_kernel(x_ref, mod_ref, w_ref, cs_ref, o_ref):
    u = _modulated(x_ref, mod_ref)
    acc = jnp.dot(u, w_ref[...], preferred_element_type=F32)
    o_ref[...] = (acc * cs_ref[...]).astype(o_ref.dtype)


def _proj_nat_call(x, mod3, w, colscale, out_dtype):
    bsz, seq, _ = x.shape
    n = w.shape[1]
    return pl.pallas_call(
        _proj_nat_kernel,
        grid=(bsz, seq // PROJ_TS, n // PROJ_TN),
        in_specs=[
            pl.BlockSpec((None, PROJ_TS, D_MODEL), lambda b, s, j: (b, s, 0)),
            pl.BlockSpec((None, 1, 3 * D_MODEL), lambda b, s, j: (b, 0, 0)),
            pl.BlockSpec((D_MODEL, PROJ_TN), lambda b, s, j: (0, j)),
            pl.BlockSpec((1, PROJ_TN), lambda b, s, j: (0, j)),
        ],
        out_specs=pl.BlockSpec((None, PROJ_TS, PROJ_TN), lambda b, s, j: (b, s, j)),
        out_shape=jax.ShapeDtypeStruct((bsz, seq, n), out_dtype),
        compiler_params=_cparams(3),
        name="proj_nat_" + jnp.dtype(out_dtype).name,
    )(x, mod3, w, colscale)


def _proj_t_kernel(x_ref, mod_ref, wt_ref, o_ref):
    u = _modulated(x_ref, mod_ref)
    acc = lax.dot_general(wt_ref[...], u, (((1,), (1,)), ((), ())),
                          preferred_element_type=F32)
    extra = V_ROWS - HEAD_W
    ones_rows = (lax.broadcasted_iota(jnp.int32, (extra, ATT_T), 0) == 0).astype(o_ref.dtype)
    for jj in range(PROJ_TS // ATT_T):
        for h in range(N_HEADS):
            o_ref[jj, h, 0:HEAD_W, :] = acc[h * HEAD_W:(h + 1) * HEAD_W,
                                            jj * ATT_T:(jj + 1) * ATT_T].astype(o_ref.dtype)
            o_ref[jj, h, HEAD_W:V_ROWS, :] = ones_rows


def _proj_t_call(x, mod3, wt):
    bsz, seq, _ = x.shape
    n = wt.shape[0]
    per = PROJ_TS // ATT_T
    return pl.pallas_call(
        _proj_t_kernel,
        grid=(bsz, seq // PROJ_TS),
        in_specs=[
            pl.BlockSpec((None, PROJ_TS, D_MODEL), lambda b, s: (b, s, 0)),
            pl.BlockSpec((None, 1, 3 * D_MODEL), lambda b, s: (b, 0, 0)),
            pl.BlockSpec((n, D_MODEL), lambda b, s: (0, 0)),
        ],
        out_specs=pl.BlockSpec((None, per, N_HEADS, V_ROWS, ATT_T), lambda b, s: (b, s, 0, 0, 0)),
        out_shape=jax.ShapeDtypeStruct((bsz, seq // ATT_T, N_HEADS, V_ROWS, ATT_T), BF16),
        compiler_params=_cparams(2),
        name="proj_t",
    )(x, mod3, wt)


def _neg_expm1(z):
    e = jnp.exp(z)
    em1 = e - 1.0
    corrected = em1 * z / jnp.log(e)
    return -jnp.where(e == 1.0, z, jnp.where(em1 == -1.0, -1.0, corrected))


def _rnn_kernel(rx_ref, rg_ref, cw_ref, cb_ref, wa_ref, ba_ref, wx_ref, bx_ref, lam_ref,
                wo_ref, o_ref, xbuf, hcar, acum, bcum):
    ts = RNN_TS

    @pl.when(pl.program_id(1) == 0)
    def _():
        xbuf[0:8, :] = jnp.zeros((8, D_MODEL), F32)
        hcar[...] = jnp.zeros((1, D_MODEL), F32)

    x = rx_ref[...]
    xbuf[8:8 + ts, :] = x
    conv = (cw_ref[3:4, :] * x
            + cw_ref[2:3, :] * xbuf[7:7 + ts, :]
            + cw_ref[1:2, :] * xbuf[6:6 + ts, :]
            + cw_ref[0:1, :] * xbuf[5:5 + ts, :]
            + cb_ref[...])
    xbuf[0:8, :] = x[ts - 8:ts, :]

    cb16 = conv.astype(BF16)
    r_parts, i_parts = [], []
    for g in range(GATE_GROUPS):
        xg = cb16[:, g * MXU_TILE:(g + 1) * MXU_TILE]
        r_parts.append(jnp.dot(xg, wa_ref[g], preferred_element_type=F32))
        i_parts.append(jnp.dot(xg, wx_ref[g], preferred_element_type=F32))
    r = _sigmoid(jnp.concatenate(r_parts, axis=1) + ba_ref[...])
    ig = _sigmoid(jnp.concatenate(i_parts, axis=1) + bx_ref[...])

    nl = -lam_ref[...]
    softplus = jnp.maximum(nl, 0.0) + jnp.log(1.0 + jnp.exp(-jnp.abs(nl)))
    log_a = (-LRU_C) * r * softplus
    a = jnp.exp(log_a)
    b = jnp.sqrt(_neg_expm1(2.0 * log_a)) * (ig * conv)

    row = lax.broadcasted_iota(jnp.int32, (ts, D_MODEL), 0) % 8
    for d in (1, 2, 4):
        keep = row >= d
        a_sh = jnp.where(keep, pltpu.roll(a, d, 0), 1.0)
        b_sh = jnp.where(keep, pltpu.roll(b, d, 0), 0.0)
        b = b + a * b_sh
        a = a * a_sh
    acum[...] = a
    bcum[...] = b

    def group(g, carry):
        r0 = pl.multiple_of(g * 8, 8)
        hg = acum[pl.ds(r0, 8), :] * carry + bcum[pl.ds(r0, 8), :]
        bcum[pl.ds(r0, 8), :] = hg
        return hg[7:8, :]

    hcar[...] = lax.fori_loop(0, ts // 8, group, hcar[...])

    y = (bcum[...] * _silu(rg_ref[...].astype(F32))).astype(BF16)
    o_ref[...] = jnp.dot(y, wo_ref[...], preferred_element_type=F32).astype(o_ref.dtype)


def _rnn_call(rx, nat, conv_w, conv_b, wa, ba, wx, bx, lam, w_out):
    bsz, seq, _ = rx.shape
    row = lambda v: v.reshape(1, D_MODEL)
    const2 = lambda b, t: (0, 0)
    return pl.pallas_call(
        _rnn_kernel,
        grid=(bsz, seq // RNN_TS),
        in_specs=[
            pl.BlockSpec((None, RNN_TS, D_MODEL), lambda b, t: (b, t, 0)),
            pl.BlockSpec((None, RNN_TS, D_MODEL), lambda b, t: (b, t, COL_RG)),
            pl.BlockSpec((CONV_WIDTH, D_MODEL), const2),
            pl.BlockSpec((1, D_MODEL), const2),
            pl.BlockSpec((GATE_GROUPS, MXU_TILE, MXU_TILE), lambda b, t: (0, 0, 0)),
            pl.BlockSpec((1, D_MODEL), const2),
            pl.BlockSpec((GATE_GROUPS, MXU_TILE, MXU_TILE), lambda b, t: (0, 0, 0)),
            pl.BlockSpec((1, D_MODEL), const2),
            pl.BlockSpec((1, D_MODEL), const2),
            pl.BlockSpec((D_MODEL, D_MODEL), const2),
        ],
        out_specs=pl.BlockSpec((None, RNN_TS, D_MODEL), lambda b, t: (b, t, 0)),
        out_shape=jax.ShapeDtypeStruct((bsz, seq, D_MODEL), F32),
        scratch_shapes=[
            pltpu.VMEM((RNN_TS + 8, D_MODEL), F32),
            pltpu.VMEM((1, D_MODEL), F32),
            pltpu.VMEM((RNN_TS, D_MODEL), F32),
            pltpu.VMEM((RNN_TS, D_MODEL), F32),
        ],
        compiler_params=_cparams(2),
        name="rnn",
    )(rx, nat, conv_w, row(conv_b), wa, row(ba), wx, row(bx), row(lam), w_out)


def _rel_bias_tile(rb_ref, head, dist):
    n = jnp.maximum(dist, 0)
    nf = jnp.maximum(n, 1).astype(F32)
    large = MAX_EXACT + (jnp.log(nf / MAX_EXACT) / math.log(MAX_DISTANCE / MAX_EXACT)
                         * (N_BUCKETS - MAX_EXACT)).astype(jnp.int32)
    large = jnp.minimum(large, N_BUCKETS - 1)
    bucket = jnp.where(n < MAX_EXACT, n, large)
    far = rb_ref[N_BUCKETS - 1, head]
    out = jnp.zeros(dist.shape, F32)
    for bkt in range(N_BUCKETS - 1):
        out = jnp.where(bucket == bkt, (rb_ref[bkt, head] - far) * LOG2E, out)
    return out


def _attn_tile_tables(nq):
    far_i = [i for i in range(2, nq) for j in range(i - 1)]
    far_j = [j for i in range(2, nq) for j in range(i - 1)]
    near = [(0, 0, 0)] + [t for i in range(1, nq) for t in ((i, i - 1, 1), (i, i, 0))]
    assert len(far_i) % ATT_FAR_UNROLL == 0 and len(near) % ATT_NEAR_UNROLL == 0
    pad_i32 = lambda v, n: jnp.asarray(list(v) + [v[-1]] * n, jnp.int32)
    return (pad_i32(far_i, ATT_FAR_UNROLL), pad_i32(far_j, ATT_FAR_UNROLL),
            pad_i32([t[0] for t in near], ATT_NEAR_UNROLL),
            pad_i32([t[1] for t in near], ATT_NEAR_UNROLL),
            pad_i32([t[2] for t in near], ATT_NEAR_UNROLL))


def _attn_kernel(far_i, far_j, near_i, near_j, near_kind,
                 rb_ref, lq1_ref, lk1_ref, lq2_ref, lk2_ref, sw_ref, q_ref, k_ref, vt_ref, ag_ref,
                 o_ref, bias2, m_all, acc_all, sbuf, tmbuf):
    t = ATT_T
    nq = q_ref.shape[0] // t
    head = pl.program_id(0)

    @pl.when(pl.program_id(1) == 0)
    def _():
        kk = lax.broadcasted_iota(jnp.int32, (t, t), 0)
        qq = lax.broadcasted_iota(jnp.int32, (t, t), 1)
        dist = qq - kk
        bias2[0] = jnp.where(dist >= 0, _rel_bias_tile(rb_ref, head, dist), -jnp.inf)
        bias2[1] = _rel_bias_tile(rb_ref, head, dist + t)

    def init(i, carry):
        m_all[i] = jnp.full((1, 2 * t), -jnp.inf, F32)
        acc_all[i] = jnp.zeros((V_ROWS, 2 * t), F32)
        return carry

    lax.fori_loop(0, nq, init, 0)

    lane = lax.broadcasted_iota(jnp.int32, (t, HEAD_W), 1)

    def stash(u, i, j, bias):
        q = q_ref[pl.ds(pl.multiple_of(i * t, t), t), :]
        zero = jnp.zeros_like(q)
        wq = jnp.concatenate([jnp.where(lane < HEAD_DIM, q, zero),
                              jnp.where(lane >= HEAD_DIM, q, zero)], axis=0)
        kj = k_ref[pl.ds(pl.multiple_of(j * t, t), t), :]
        s = lax.dot_general(kj, wq, (((1,), (1,)), ((), ())), preferred_element_type=F32)
        if bias is not None:
            s = s + jnp.concatenate([bias, bias], axis=1)
        sbuf[u] = s
        tmbuf[u] = jnp.max(s, axis=0, keepdims=True)

    def consume(u, i, j):
        m_old = m_all[i]
        m_new = jnp.maximum(m_old, tmbuf[u])
        m_all[i] = m_new
        p = jnp.exp2(sbuf[u] - m_new).astype(BF16)
        alpha = jnp.exp2(m_old - m_new)
        pv = jnp.dot(vt_ref[j], p, preferred_element_type=F32)
        acc_all[i] = alpha * acc_all[i] + pv

    def run_tiles(tab_i, tab_j, tab_kind, unroll):
        n_groups = tab_i.shape[0] // unroll - 1
        bias_of = (lambda n: None) if tab_kind is None else (lambda n: bias2[tab_kind[n]])
        for u in range(unroll):
            stash(u, tab_i[u], tab_j[u], bias_of(u))

        def group(g, carry):
            for u in range(unroll):
                n = g * unroll + u
                nxt = n + unroll
                consume(u, tab_i[n], tab_j[n])
                stash(u, tab_i[nxt], tab_j[nxt], bias_of(nxt))
            return carry

        lax.fori_loop(0, n_groups, group, 0)

    run_tiles(far_i, far_j, None, ATT_FAR_UNROLL)
    run_tiles(near_i, near_j, near_kind, ATT_NEAR_UNROLL)

    lam = (jnp.exp(jnp.sum(lq1_ref[...] * lk1_ref[...], axis=1, keepdims=True))
           - jnp.exp(jnp.sum(lq2_ref[...] * lk2_ref[...], axis=1, keepdims=True))
           + LAMBDA_INIT)

    def finish(i, carry):
        acc = acc_all[i]
        on = acc[0:HEAD_W, :] * (1.0 / acc[HEAD_W:HEAD_W + 1, :])
        o = on[:, :t] - lam * on[:, t:]
        ms = jnp.mean(o * o, axis=0, keepdims=True)
        o = (o * lax.rsqrt(ms + LN_EPS)).T
        o = o * sw_ref[...] * (1.0 - LAMBDA_INIT)
        rows = pl.ds(pl.multiple_of(i * t, t), t)
        o_ref[rows, :] = (o * _silu(ag_ref[rows, :].astype(F32))).astype(o_ref.dtype)
        return carry

    lax.fori_loop(0, nq, finish, 0)


def _attn_call(nat, vt, rel_bias, lq1, lk1, lq2, lk2, subln_w):
    bsz, seq, _ = nat.shape
    nt = seq // ATT_T
    hb = D_MODEL // HEAD_W
    vec = lambda v: v.reshape(1, -1)
    const2 = lambda h, b, *_: (0, 0)
    col = lambda c: (lambda h, b, *_: (b, 0, c * hb + h))
    tables = _attn_tile_tables(nt)
    grid_spec = pltpu.PrefetchScalarGridSpec(
        num_scalar_prefetch=len(tables),
        grid=(N_HEADS, bsz),
        in_specs=[
            pl.BlockSpec(memory_space=pltpu.SMEM),
            pl.BlockSpec((1, HEAD_DIM), const2),
            pl.BlockSpec((1, HEAD_DIM), const2),
            pl.BlockSpec((1, HEAD_DIM), const2),
            pl.BlockSpec((1, HEAD_DIM), const2),
            pl.BlockSpec((1, HEAD_W), const2),
            pl.BlockSpec((None, seq, HEAD_W), col(COL_Q)),
            pl.BlockSpec((None, seq, HEAD_W), col(COL_K)),
            pl.BlockSpec((None, nt, None, V_ROWS, ATT_T), lambda h, b, *_: (b, 0, h, 0, 0)),
            pl.BlockSpec((None, seq, HEAD_W), col(COL_AG)),
        ],
        out_specs=pl.BlockSpec((None, seq, HEAD_W), lambda h, b, *_: (b, 0, h)),
        scratch_shapes=[
            pltpu.VMEM((2, ATT_T, ATT_T), F32),
            pltpu.VMEM((nt, 1, 2 * ATT_T), F32),
            pltpu.VMEM((nt, V_ROWS, 2 * ATT_T), F32),
            pltpu.VMEM((max(ATT_FAR_UNROLL, ATT_NEAR_UNROLL), ATT_T, 2 * ATT_T), F32),
            pltpu.VMEM((max(ATT_FAR_UNROLL, ATT_NEAR_UNROLL), 1, 2 * ATT_T), F32),
        ],
    )
    return pl.pallas_call(
        _attn_kernel,
        grid_spec=grid_spec,
        out_shape=jax.ShapeDtypeStruct((bsz, seq, D_MODEL), BF16),
        compiler_params=pltpu.CompilerParams(
            dimension_semantics=("arbitrary", "arbitrary"),
            vmem_limit_bytes=ATT_VMEM_LIMIT,
        ),
        name="attn",
    )(*tables, rel_bias, vec(lq1), vec(lk1), vec(lq2), vec(lk2), vec(subln_w), nat, nat, vt, nat)


def _final_kernel(x_ref, mod_ref, yr_ref, ya_ref, mg_ref, wa_ref, wo_ref, g_ref, b_ref, o_ref):
    gate = mod_ref[:, 2 * D_MODEL:3 * D_MODEL]
    y_attn = jnp.dot(ya_ref[...], wa_ref[...], preferred_element_type=F32)
    mg = _sigmoid(mg_ref[...].astype(F32))
    merged = mg[:, :D_MODEL] * yr_ref[...] + mg[:, D_MODEL:] * y_attn
    out = jnp.dot(merged.astype(BF16), wo_ref[...], preferred_element_type=F32) * gate
    z = DEEPNORM_ALPHA * x_ref[...] + out
    mu = jnp.mean(z, axis=-1, keepdims=True)
    zc = z - mu
    var = jnp.mean(zc * zc, axis=-1, keepdims=True)
    o_ref[...] = zc * lax.rsqrt(var + LN_EPS) * g_ref[...] + b_ref[...]


def _final_call(x, mod3, y_rnn, y_attn_pre, nat, w_attn_out, w_out, ln_g, ln_b):
    bsz, seq, _ = x.shape
    row = lambda v: v.reshape(1, D_MODEL)
    const2 = lambda b, s: (0, 0)
    tile = lambda: pl.BlockSpec((None, FIN_TS, D_MODEL), lambda b, s: (b, s, 0))
    return pl.pallas_call(
        _final_kernel,
        grid=(bsz, seq // FIN_TS),
        in_specs=[
            tile(),
            pl.BlockSpec((None, 1, 3 * D_MODEL), lambda b, s: (b, 0, 0)),
            tile(),
            tile(),
            pl.BlockSpec((None, FIN_TS, 2 * D_MODEL), lambda b, s: (b, s, COL_MG // 2)),
            pl.BlockSpec((D_MODEL, D_MODEL), const2),
            pl.BlockSpec((D_MODEL, D_MODEL), const2),
            pl.BlockSpec((1, D_MODEL), const2),
            pl.BlockSpec((1, D_MODEL), const2),
        ],
        out_specs=tile(),
        out_shape=jax.ShapeDtypeStruct((bsz, seq, D_MODEL), F32),
        compiler_params=_cparams(2),
        name="final",
    )(x, mod3, y_rnn, y_attn_pre, nat, w_attn_out, w_out, row(ln_g), row(ln_b))


def _block_diag_groups(w):
    per = MXU_TILE // RNN_BLOCK
    w = w.reshape(GATE_GROUPS, per, RNN_BLOCK, RNN_BLOCK)
    eye = jnp.eye(per, dtype=w.dtype)
    return jnp.einsum("gaij,ab->gaibj", w, eye).reshape(GATE_GROUPS, MXU_TILE, MXU_TILE)


def kernel(x, c, w_ada, b_ada, w_in, conv_w, conv_b, w_rg_a, b_rg_a, w_rg_x, b_rg_x, lru_lambda,
           lambda_q1, lambda_k1, lambda_q2, lambda_k2, subln_w, rel_bias, w_rnn_out, w_attn_out,
           w_out, ln_g, ln_b):
    layer = 0
    bsz = x.shape[0]
    d = D_MODEL

    mod3 = _mod_call(c, w_ada[layer], b_ada[layer]).reshape(bsz, 1, 3 * d)

    w_in16 = w_in[layer].astype(BF16)
    w_rx = w_in16[:, 0:d]
    w_nat = jnp.concatenate([w_in16[:, d:4 * d], w_in16[:, 5 * d:8 * d]], axis=1)
    w_vt = w_in16[:, 4 * d:5 * d].T
    ones = jnp.ones((1, d), F32)
    colscale = jnp.concatenate(
        [ones, jnp.full((1, d), LOG2E * HEAD_DIM ** -0.5, F32), ones, ones, ones, ones], axis=1)

    rx = _proj_nat_call(x, mod3, w_rx, ones, F32)
    nat = _proj_nat_call(x, mod3, w_nat, colscale, BF16)
    vt = _proj_t_call(x, mod3, w_vt)

    y_rnn = _rnn_call(rx, nat, conv_w[layer], conv_b[layer],
                      _block_diag_groups(w_rg_a[layer]).astype(BF16), b_rg_a[layer],
                      _block_diag_groups(w_rg_x[layer]).astype(BF16), b_rg_x[layer],
                      lru_lambda[layer], w_rnn_out[layer].astype(BF16))

    y_attn_pre = _attn_call(nat, vt, rel_bias, lambda_q1[layer], lambda_k1[layer],
                            lambda_q2[layer], lambda_k2[layer], subln_w[layer])

    return _final_call(x, mod3, y_rnn, y_attn_pre, nat, w_attn_out[layer].astype(BF16),
                       w_out[layer].astype(BF16), ln_g[layer], ln_b[layer])
```

```python
import functools
import math

import jax
import jax.numpy as jnp
import numpy as np
from jax import lax
from jax.experimental import pallas as pl
from jax.experimental.pallas import tpu as pltpu

F32 = jnp.float32
BF16 = jnp.bfloat16

D_MODEL = 1024
N_RNN_BLOCKS = 16
RNN_BLOCK = D_MODEL // N_RNN_BLOCKS
CONV_WIDTH = 4
LRU_C = 8.0
N_HEADS = 8
HEAD_DIM = 64
HEAD_W = 2 * HEAD_DIM
N_BUCKETS = 32
MAX_EXACT = N_BUCKETS // 2
MAX_DISTANCE = 128
LN_EPS = 1e-5
DEPTH = 1
DEEPNORM_ALPHA = (2.0 * DEPTH) ** 0.25
LAMBDA_INIT = 0.8 - 0.6 * math.exp(-0.3 * 0)

MXU_TILE = 256
GATE_GROUPS = D_MODEL // MXU_TILE

LIN_RX, LIN_Q, LIN_K = 0, 1, 2
GATE_RG, GATE_AG = 0, 1

PROJ_TS = 1024
PROJ_TN = 1024
RNN_TS = 512
RNN_SEG = RNN_TS // 8
RSQRT_FLOOR = 1e-30
ATT_T = 256
ATT_FAR_UNROLL = 15
ATT_NEAR_UNROLL = 9
V_ROWS = HEAD_W + 16
LOG2E = 1.4426950408889634
FIN_TS = 512

VMEM_LIMIT = 48 * 1024 * 1024
ATT_VMEM_LIMIT = 56 * 1024 * 1024


def _cparams(n_axes):
    return pltpu.CompilerParams(
        dimension_semantics=("arbitrary",) * n_axes,
        vmem_limit_bytes=VMEM_LIMIT,
    )


def _sigmoid(z):
    return 1.0 / (1.0 + jnp.exp(-z))


def _silu(z):
    return z * _sigmoid(z)


def _mod_kernel(c_ref, w_ref, b_ref, o_ref):
    c = c_ref[...]
    o_ref[...] = jnp.dot(_silu(c), w_ref[...], preferred_element_type=F32,
                         precision=lax.Precision.HIGHEST) + b_ref[...]


def _mod_call(c, w_ada, b_ada):
    bsz = c.shape[0]
    n = w_ada.shape[1]
    tn = 1024
    return pl.pallas_call(
        _mod_kernel,
        grid=(n // tn,),
        in_specs=[
            pl.BlockSpec((bsz, D_MODEL), lambda j: (0, 0)),
            pl.BlockSpec((D_MODEL, tn), lambda j: (0, j)),
            pl.BlockSpec((1, tn), lambda j: (0, j)),
        ],
        out_specs=pl.BlockSpec((bsz, tn), lambda j: (0, j)),
        out_shape=jax.ShapeDtypeStruct((bsz, n), F32),
        compiler_params=_cparams(1),
        name="mod",
    )(c, w_ada, b_ada.reshape(1, n))


def _modulated(x_ref, mod_ref):
    shift = mod_ref[:, 0:D_MODEL]
    scale = mod_ref[:, D_MODEL:2 * D_MODEL]
    return (x_ref[...] * (1.0 + scale) + shift).astype(BF16)


def _proj_nat_kernel(x_ref, mod_ref, w_ref, cs_ref, o_ref, *, act):
    u = _modulated(x_ref, mod_ref)
    acc = jnp.dot(u, w_ref[...], preferred_element_type=F32) * cs_ref[...]
    if act == "silu":
        acc = acc * _sigmoid2(acc)
    elif act == "sigmoid":
        acc = _sigmoid2(acc)
    o_ref[...] = acc.astype(o_ref.dtype)


def _proj_nat_call(x, mod3, w, colscale, act):
    bsz, seq, _ = x.shape
    n = w.shape[1]
    out_dtype = BF16
    return pl.pallas_call(
        functools.partial(_proj_nat_kernel, act=act),
        grid=(bsz, seq // PROJ_TS, n // PROJ_TN),
        in_specs=[
            pl.BlockSpec((None, PROJ_TS, D_MODEL), lambda b, s, j: (b, s, 0)),
            pl.BlockSpec((None, 1, 3 * D_MODEL), lambda b, s, j: (b, 0, 0)),
            pl.BlockSpec((D_MODEL, PROJ_TN), lambda b, s, j: (0, j)),
            pl.BlockSpec((1, PROJ_TN), lambda b, s, j: (0, j)),
        ],
        out_specs=pl.BlockSpec((None, PROJ_TS, PROJ_TN), lambda b, s, j: (b, s, j)),
        out_shape=jax.ShapeDtypeStruct((bsz, seq, n), out_dtype),
        compiler_params=_cparams(3),
        name="proj_" + act,
    )(x, mod3, w, colscale)


def _proj_t_kernel(x_ref, mod_ref, wt_ref, o_ref):
    u = _modulated(x_ref, mod_ref)
    acc = lax.dot_general(wt_ref[...], u, (((1,), (1,)), ((), ())),
                          preferred_element_type=F32)
    extra = V_ROWS - HEAD_W
    ones_rows = (lax.broadcasted_iota(jnp.int32, (extra, ATT_T), 0) == 0).astype(o_ref.dtype)
    for jj in range(PROJ_TS // ATT_T):
        for h in range(N_HEADS):
            o_ref[jj, h, 0:HEAD_W, :] = acc[h * HEAD_W:(h + 1) * HEAD_W,
                                            jj * ATT_T:(jj + 1) * ATT_T].astype(o_ref.dtype)
            o_ref[jj, h, HEAD_W:V_ROWS, :] = ones_rows


def _proj_t_call(x, mod3, wt):
    bsz, seq, _ = x.shape
    n = wt.shape[0]
    per = PROJ_TS // ATT_T
    return pl.pallas_call(
        _proj_t_kernel,
        grid=(bsz, seq // PROJ_TS),
        in_specs=[
            pl.BlockSpec((None, PROJ_TS, D_MODEL), lambda b, s: (b, s, 0)),
            pl.BlockSpec((None, 1, 3 * D_MODEL), lambda b, s: (b, 0, 0)),
            pl.BlockSpec((n, D_MODEL), lambda b, s: (0, 0)),
        ],
        out_specs=pl.BlockSpec((None, per, N_HEADS, V_ROWS, ATT_T), lambda b, s: (b, s, 0, 0, 0)),
        out_shape=jax.ShapeDtypeStruct((bsz, seq // ATT_T, N_HEADS, V_ROWS, ATT_T), BF16),
        compiler_params=_cparams(2),
        name="proj_t",
    )(x, mod3, wt)


def _rnn_row_permutation():
    rho = np.arange(RNN_TS)
    pm = np.zeros((RNN_TS, RNN_TS), np.float32)
    pm[rho, (rho % 8) * RNN_SEG + rho // 8] = 1.0
    return jnp.asarray(pm, BF16)


def _sigmoid2(z):
    return 1.0 / (1.0 + jnp.exp2(z * (-LOG2E)))


def _rnn_kernel(first_ref, next_ref, gprev_ref, pm_ref, pmt_ref, cw_ref, cb_ref, wa_ref, ba_ref,
                wx_ref, bx_ref, lam_ref, wo_ref, o_ref, tail, hcar, xp_s, hp_s):
    ts = RNN_TS
    n_groups = ts // 8
    d_model = D_MODEL

    @pl.when(pl.program_id(1) == 0)
    def _():
        tail[...] = jnp.zeros(tail.shape, F32)
        hcar[...] = jnp.zeros((1, d_model), F32)
        hp_s[...] = jnp.zeros(hp_s.shape, hp_s.dtype)
        xp_s[...] = jnp.dot(pm_ref[...], first_ref[...], preferred_element_type=F32)

    sub = lax.broadcasted_iota(jnp.int32, (8, d_model), 0)
    nl = -lam_ref[...]
    softplus = jnp.maximum(nl, 0.0) + jnp.log(1.0 + jnp.exp(-jnp.abs(nl)))
    neg_rate = LRU_C * softplus
    rate2 = neg_rate * (-LOG2E)

    def conv_of(xp, prev):
        last = xp[ts - 8 * (CONV_WIDTH - 1):ts, :]
        lead = []
        for k in range(CONV_WIDTH - 1):
            cur_k = pltpu.roll(last[8 * k:8 * k + 8, :], 1, 0)
            prev_k = pltpu.roll(prev[8 * k:8 * k + 8, :], 1, 0)
            lead.append(jnp.where(sub == 0, prev_k, cur_k))
        xe = jnp.concatenate(lead + [xp], axis=0)
        conv = cb_ref[...] + cw_ref[CONV_WIDTH - 1:CONV_WIDTH, :] * xp
        for d in range(1, CONV_WIDTH):
            r0 = 8 * (CONV_WIDTH - 1 - d)
            conv = conv + cw_ref[CONV_WIDTH - 1 - d:CONV_WIDTH - d, :] * xe[r0:r0 + ts, :]
        return conv, last

    def gates_of(conv, y_prev):
        cb16 = conv.astype(BF16)
        r_parts, i_parts = [], []
        for g in range(GATE_GROUPS):
            cols = slice(g * MXU_TILE, (g + 1) * MXU_TILE)
            xg = cb16[:, cols]
            r_parts.append(jnp.dot(xg, wa_ref[g], preferred_element_type=F32))
            i_parts.append(jnp.dot(xg, wx_ref[g], preferred_element_type=F32))
            o_ref[:, cols] = jnp.dot(y_prev, wo_ref[:, cols],
                                     preferred_element_type=F32).astype(o_ref.dtype)
        return jnp.concatenate(r_parts, axis=1), jnp.concatenate(i_parts, axis=1)

    def scan_of(conv, r_pre, i_pre, state):
        r = 1.0 / (1.0 + jnp.exp2(r_pre + ba_ref[...]))
        ig = 1.0 / (1.0 + jnp.exp2(i_pre + bx_ref[...]))
        a = jnp.exp2(r * rate2)
        one_minus_a2 = jnp.tanh(r * neg_rate) * (1.0 + a * a)
        root = one_minus_a2 * lax.rsqrt(jnp.maximum(one_minus_a2, RSQRT_FLOOR))
        b = root * (ig * conv)
        seg_end = jnp.zeros((8, d_model), F32)
        seg_prod = jnp.ones((8, d_model), F32)
        for g in range(n_groups):
            ag = a[8 * g:8 * g + 8, :]
            seg_end = ag * seg_end + b[8 * g:8 * g + 8, :]
            seg_prod = ag * seg_prod
        seg_in = jnp.zeros((8, d_model), F32)
        for seg in range(8):
            seg_in = jnp.where(sub == seg, state, seg_in)
            state = seg_end[seg:seg + 1, :] + seg_prod[seg:seg + 1, :] * state
        h = seg_in
        hs = []
        for g in range(n_groups):
            h = a[8 * g:8 * g + 8, :] * h + b[8 * g:8 * g + 8, :]
            hs.append(h)
        return jnp.concatenate(hs, axis=0), state

    h_prev = jnp.dot(pmt_ref[...], hp_s[...], preferred_element_type=F32)
    y_prev = (h_prev * gprev_ref[...].astype(F32)).astype(BF16)

    conv, last = conv_of(xp_s[...], tail[...])
    tail[...] = last
    r_pre, i_pre = gates_of(conv, y_prev)

    hp, state = scan_of(conv, r_pre, i_pre, hcar[...])
    hcar[...] = state
    hp_s[...] = hp.astype(BF16)

    xp_s[...] = jnp.dot(pm_ref[...], next_ref[...], preferred_element_type=F32)


def _rnn_call(lin, gates, conv_w, conv_b, wa, ba, wx, bx, lam, w_out):
    bsz, seq, _ = lin.shape
    row = lambda v: v.reshape(1, D_MODEL)
    const2 = lambda b, t: (0, 0)
    pm = _rnn_row_permutation()
    rows = RNN_TS
    n_tiles = seq // rows
    return pl.pallas_call(
        _rnn_kernel,
        grid=(bsz, n_tiles + 1),
        in_specs=[
            pl.BlockSpec((None, rows, D_MODEL), lambda b, t: (b, 0, LIN_RX)),
            pl.BlockSpec((None, rows, D_MODEL),
                         lambda b, t: (b, jnp.minimum(t + 1, n_tiles - 1), LIN_RX)),
            pl.BlockSpec((None, rows, D_MODEL), lambda b, t: (b, jnp.maximum(t - 1, 0), GATE_RG)),
            pl.BlockSpec((RNN_TS, RNN_TS), const2),
            pl.BlockSpec((RNN_TS, RNN_TS), const2),
            pl.BlockSpec((CONV_WIDTH, D_MODEL), const2),
            pl.BlockSpec((1, D_MODEL), const2),
            pl.BlockSpec((GATE_GROUPS, MXU_TILE, MXU_TILE), lambda b, t: (0, 0, 0)),
            pl.BlockSpec((1, D_MODEL), const2),
            pl.BlockSpec((GATE_GROUPS, MXU_TILE, MXU_TILE), lambda b, t: (0, 0, 0)),
            pl.BlockSpec((1, D_MODEL), const2),
            pl.BlockSpec((1, D_MODEL), const2),
            pl.BlockSpec((D_MODEL, D_MODEL), const2),
        ],
        out_specs=pl.BlockSpec((None, rows, D_MODEL), lambda b, t: (b, jnp.maximum(t - 1, 0), 0)),
        out_shape=jax.ShapeDtypeStruct((bsz, seq, D_MODEL), BF16),
        scratch_shapes=[
            pltpu.VMEM((8 * (CONV_WIDTH - 1), D_MODEL), F32),
            pltpu.VMEM((1, D_MODEL), F32),
            pltpu.VMEM((rows, D_MODEL), F32),
            pltpu.VMEM((rows, D_MODEL), BF16),
        ],
        compiler_params=_cparams(2),
        name="rnn",
    )(lin, lin, gates, pm, pm.T, conv_w, row(conv_b), wa, row(ba), wx, row(bx), row(lam), w_out)


def _rel_bias_tile(rb_ref, head, dist):
    n = jnp.maximum(dist, 0)
    nf = jnp.maximum(n, 1).astype(F32)
    large = MAX_EXACT + (jnp.log(nf / MAX_EXACT) / math.log(MAX_DISTANCE / MAX_EXACT)
                         * (N_BUCKETS - MAX_EXACT)).astype(jnp.int32)
    large = jnp.minimum(large, N_BUCKETS - 1)
    bucket = jnp.where(n < MAX_EXACT, n, large)
    far = rb_ref[N_BUCKETS - 1, head]
    out = jnp.zeros(dist.shape, F32)
    for bkt in range(N_BUCKETS - 1):
        out = jnp.where(bucket == bkt, (rb_ref[bkt, head] - far) * LOG2E, out)
    return out


def _attn_tile_tables(nq):
    far_i = [i for i in range(2, nq) for j in range(i - 1)]
    far_j = [j for i in range(2, nq) for j in range(i - 1)]
    near = [(0, 0, 0)] + [t for i in range(1, nq) for t in ((i, i - 1, 1), (i, i, 0))]
    assert len(far_i) % ATT_FAR_UNROLL == 0 and len(near) % ATT_NEAR_UNROLL == 0
    pad_i32 = lambda v, n: jnp.asarray(list(v) + [v[-1]] * n, jnp.int32)
    return (pad_i32(far_i, ATT_FAR_UNROLL), pad_i32(far_j, ATT_FAR_UNROLL),
            pad_i32([t[0] for t in near], ATT_NEAR_UNROLL),
            pad_i32([t[1] for t in near], ATT_NEAR_UNROLL),
            pad_i32([t[2] for t in near], ATT_NEAR_UNROLL))


def _attn_kernel(far_i, far_j, near_i, near_j, near_kind,
                 rb_ref, lq1_ref, lk1_ref, lq2_ref, lk2_ref, sw_ref, q_ref, k_ref, vt_ref, ag_ref,
                 o_ref, bias2, m_all, acc_all, sbuf, tmbuf):
    t = ATT_T
    nq = q_ref.shape[0] // t
    head = pl.program_id(0)

    @pl.when(pl.program_id(1) == 0)
    def _():
        kk = lax.broadcasted_iota(jnp.int32, (t, t), 0)
        qq = lax.broadcasted_iota(jnp.int32, (t, t), 1)
        dist = qq - kk
        bias2[0] = jnp.where(dist >= 0, _rel_bias_tile(rb_ref, head, dist), -jnp.inf)
        bias2[1] = _rel_bias_tile(rb_ref, head, dist + t)

    def init(i, carry):
        m_all[i] = jnp.full((1, 2 * t), -jnp.inf, F32)
        acc_all[i] = jnp.zeros((V_ROWS, 2 * t), F32)
        return carry

    lax.fori_loop(0, nq, init, 0)

    lane = lax.broadcasted_iota(jnp.int32, (t, HEAD_W), 1)

    def stash(u, i, j, bias):
        q = q_ref[pl.ds(pl.multiple_of(i * t, t), t), :]
        zero = jnp.zeros_like(q)
        wq = jnp.concatenate([jnp.where(lane < HEAD_DIM, q, zero),
                              jnp.where(lane >= HEAD_DIM, q, zero)], axis=0)
        kj = k_ref[pl.ds(pl.multiple_of(j * t, t), t), :]
        s = lax.dot_general(kj, wq, (((1,), (1,)), ((), ())), preferred_element_type=F32)
        if bias is not None:
            s = s + jnp.concatenate([bias, bias], axis=1)
        sbuf[u] = s
        tmbuf[u] = jnp.max(s, axis=0, keepdims=True)

    def consume(u, i, j):
        m_old = m_all[i]
        m_new = jnp.maximum(m_old, tmbuf[u])
        m_all[i] = m_new
        p = jnp.exp2(sbuf[u] - m_new).astype(BF16)
        alpha = jnp.exp2(m_old - m_new)
        pv = jnp.dot(vt_ref[j], p, preferred_element_type=F32)
        acc_all[i] = alpha * acc_all[i] + pv

    def run_tiles(tab_i, tab_j, tab_kind, unroll):
        n_groups = tab_i.shape[0] // unroll - 1
        bias_of = (lambda n: None) if tab_kind is None else (lambda n: bias2[tab_kind[n]])
        for u in range(unroll):
            stash(u, tab_i[u], tab_j[u], bias_of(u))

        def group(g, carry):
            for u in range(unroll):
                n = g * unroll + u
                nxt = n + unroll
                consume(u, tab_i[n], tab_j[n])
                stash(u, tab_i[nxt], tab_j[nxt], bias_of(nxt))
            return carry

        lax.fori_loop(0, n_groups, group, 0)

    run_tiles(far_i, far_j, None, ATT_FAR_UNROLL)
    run_tiles(near_i, near_j, near_kind, ATT_NEAR_UNROLL)

    lam = (jnp.exp(jnp.sum(lq1_ref[...] * lk1_ref[...], axis=1, keepdims=True))
           - jnp.exp(jnp.sum(lq2_ref[...] * lk2_ref[...], axis=1, keepdims=True))
           + LAMBDA_INIT)

    def finish(i, carry):
        acc = acc_all[i]
        on = acc[0:HEAD_W, :] * (1.0 / acc[HEAD_W:HEAD_W + 1, :])
        o = on[:, :t] - lam * on[:, t:]
        ms = jnp.mean(o * o, axis=0, keepdims=True)
        o = (o * lax.rsqrt(ms + LN_EPS)).T
        o = o * sw_ref[...] * (1.0 - LAMBDA_INIT)
        rows = pl.ds(pl.multiple_of(i * t, t), t)
        o_ref[rows, :] = (o * ag_ref[rows, :].astype(F32)).astype(o_ref.dtype)
        return carry

    lax.fori_loop(0, nq, finish, 0)


def _attn_call(lin, gates, vt, rel_bias, lq1, lk1, lq2, lk2, subln_w):
    bsz, seq, _ = lin.shape
    nt = seq // ATT_T
    hb = D_MODEL // HEAD_W
    vec = lambda v: v.reshape(1, -1)
    const2 = lambda h, b, *_: (0, 0)
    col = lambda c: (lambda h, b, *_: (b, 0, c * hb + h))
    tables = _attn_tile_tables(nt)
    grid_spec = pltpu.PrefetchScalarGridSpec(
        num_scalar_prefetch=len(tables),
        grid=(N_HEADS, bsz),
        in_specs=[
            pl.BlockSpec(memory_space=pltpu.SMEM),
            pl.BlockSpec((1, HEAD_DIM), const2),
            pl.BlockSpec((1, HEAD_DIM), const2),
            pl.BlockSpec((1, HEAD_DIM), const2),
            pl.BlockSpec((1, HEAD_DIM), const2),
            pl.BlockSpec((1, HEAD_W), const2),
            pl.BlockSpec((None, seq, HEAD_W), col(LIN_Q)),
            pl.BlockSpec((None, seq, HEAD_W), col(LIN_K)),
            pl.BlockSpec((None, nt, None, V_ROWS, ATT_T), lambda h, b, *_: (b, 0, h, 0, 0)),
            pl.BlockSpec((None, seq, HEAD_W), col(GATE_AG)),
        ],
        out_specs=pl.BlockSpec((None, seq, HEAD_W), lambda h, b, *_: (b, 0, h)),
        scratch_shapes=[
            pltpu.VMEM((2, ATT_T, ATT_T), F32),
            pltpu.VMEM((nt, 1, 2 * ATT_T), F32),
            pltpu.VMEM((nt, V_ROWS, 2 * ATT_T), F32),
            pltpu.VMEM((max(ATT_FAR_UNROLL, ATT_NEAR_UNROLL), ATT_T, 2 * ATT_T), F32),
            pltpu.VMEM((max(ATT_FAR_UNROLL, ATT_NEAR_UNROLL), 1, 2 * ATT_T), F32),
        ],
    )
    return pl.pallas_call(
        _attn_kernel,
        grid_spec=grid_spec,
        out_shape=jax.ShapeDtypeStruct((bsz, seq, D_MODEL), BF16),
        compiler_params=pltpu.CompilerParams(
            dimension_semantics=("arbitrary", "arbitrary"),
            vmem_limit_bytes=ATT_VMEM_LIMIT,
        ),
        name="attn",
    )(*tables, rel_bias, vec(lq1), vec(lk1), vec(lq2), vec(lk2), vec(subln_w), lin, lin, vt, gates)


def _final_kernel(x_ref, mod_ref, yr_ref, ya_ref, mg_ref, wa_ref, wo_ref, g_ref, b_ref, o_ref):
    gate = mod_ref[:, 2 * D_MODEL:3 * D_MODEL]
    y_attn = jnp.dot(ya_ref[...], wa_ref[...], preferred_element_type=F32)
    mg = mg_ref[...].astype(F32)
    merged = mg[:, :D_MODEL] * yr_ref[...].astype(F32) + mg[:, D_MODEL:] * y_attn
    out = jnp.dot(merged.astype(BF16), wo_ref[...], preferred_element_type=F32) * gate
    z = DEEPNORM_ALPHA * x_ref[...] + out
    mu = jnp.mean(z, axis=-1, keepdims=True)
    zc = z - mu
    var = jnp.mean(zc * zc, axis=-1, keepdims=True)
    o_ref[...] = zc * lax.rsqrt(var + LN_EPS) * g_ref[...] + b_ref[...]


def _final_call(x, mod3, y_rnn, y_attn_pre, merge, w_attn_out, w_out, ln_g, ln_b):
    bsz, seq, _ = x.shape
    row = lambda v: v.reshape(1, D_MODEL)
    const2 = lambda b, s: (0, 0)
    tile = lambda: pl.BlockSpec((None, FIN_TS, D_MODEL), lambda b, s: (b, s, 0))
    return pl.pallas_call(
        _final_kernel,
        grid=(bsz, seq // FIN_TS),
        in_specs=[
            tile(),
            pl.BlockSpec((None, 1, 3 * D_MODEL), lambda b, s: (b, 0, 0)),
            tile(),
            tile(),
            pl.BlockSpec((None, FIN_TS, 2 * D_MODEL), lambda b, s: (b, s, 0)),
            pl.BlockSpec((D_MODEL, D_MODEL), const2),
            pl.BlockSpec((D_MODEL, D_MODEL), const2),
            pl.BlockSpec((1, D_MODEL), const2),
            pl.BlockSpec((1, D_MODEL), const2),
        ],
        out_specs=tile(),
        out_shape=jax.ShapeDtypeStruct((bsz, seq, D_MODEL), F32),
        compiler_params=_cparams(2),
        name="final",
    )(x, mod3, y_rnn, y_attn_pre, merge, w_attn_out, w_out, row(ln_g), row(ln_b))


def _block_diag_groups(w):
    per = MXU_TILE // RNN_BLOCK
    w = w.reshape(GATE_GROUPS, per, RNN_BLOCK, RNN_BLOCK)
    eye = jnp.eye(per, dtype=w.dtype)
    return jnp.einsum("gaij,ab->gaibj", w, eye).reshape(GATE_GROUPS, MXU_TILE, MXU_TILE)


def kernel(x, c, w_ada, b_ada, w_in, conv_w, conv_b, w_rg_a, b_rg_a, w_rg_x, b_rg_x, lru_lambda,
           lambda_q1, lambda_k1, lambda_q2, lambda_k2, subln_w, rel_bias, w_rnn_out, w_attn_out,
           w_out, ln_g, ln_b):
    layer = 0
    bsz = x.shape[0]
    d = D_MODEL

    mod3 = _mod_call(c, w_ada[layer], b_ada[layer]).reshape(bsz, 1, 3 * d)

    w_in16 = w_in[layer].astype(BF16)
    w_lin = jnp.concatenate([w_in16[:, 0:d], w_in16[:, 2 * d:4 * d]], axis=1)
    w_gate = jnp.concatenate([w_in16[:, d:2 * d], w_in16[:, 5 * d:6 * d]], axis=1)
    w_merge = w_in16[:, 6 * d:8 * d]
    w_vt = w_in16[:, 4 * d:5 * d].T
    ones = jnp.ones((1, d), F32)
    q_scale = jnp.full((1, d), LOG2E * HEAD_DIM ** -0.5, F32)

    lin = _proj_nat_call(x, mod3, w_lin, jnp.concatenate([ones, q_scale, ones], axis=1), "none")
    gates = _proj_nat_call(x, mod3, w_gate, jnp.concatenate([ones, ones], axis=1), "silu")
    merge = _proj_nat_call(x, mod3, w_merge, jnp.concatenate([ones, ones], axis=1), "sigmoid")
    vt = _proj_t_call(x, mod3, w_vt)

    y_rnn = _rnn_call(lin, gates, conv_w[layer], conv_b[layer],
                      _block_diag_groups(w_rg_a[layer] * (-LOG2E)).astype(BF16),
                      b_rg_a[layer] * (-LOG2E),
                      _block_diag_groups(w_rg_x[layer] * (-LOG2E)).astype(BF16),
                      b_rg_x[layer] * (-LOG2E),
                      lru_lambda[layer], w_rnn_out[layer].astype(BF16))

    y_attn_pre = _attn_call(lin, gates, vt, rel_bias, lambda_q1[layer], lambda_k1[layer],
                            lambda_q2[layer], lambda_k2[layer], subln_w[layer])

    return _final_call(x, mod3, y_rnn, y_attn_pre, merge, w_attn_out[layer].astype(BF16),
                       w_out[layer].astype(BF16), ln_g[layer], ln_b[layer])
```

```python
import functools
import math

import jax
import jax.numpy as jnp
import numpy as np
from jax import lax
from jax.experimental import pallas as pl
from jax.experimental.pallas import tpu as pltpu

F32 = jnp.float32
BF16 = jnp.bfloat16

D_MODEL = 1024
N_RNN_BLOCKS = 16
RNN_BLOCK = D_MODEL // N_RNN_BLOCKS
CONV_WIDTH = 4
LRU_C = 8.0
N_HEADS = 8
HEAD_DIM = 64
HEAD_W = 2 * HEAD_DIM
N_BUCKETS = 32
MAX_EXACT = N_BUCKETS // 2
MAX_DISTANCE = 128
LN_EPS = 1e-5
DEPTH = 1
DEEPNORM_ALPHA = (2.0 * DEPTH) ** 0.25
LAMBDA_INIT = 0.8 - 0.6 * math.exp(-0.3 * 0)

MXU_TILE = 256
GATE_GROUPS = D_MODEL // MXU_TILE

LIN_RX, LIN_Q, LIN_K = 0, 1, 2
GATE_RG, GATE_AG = 0, 1

PROJ_TS = 1024
PROJ_TN = 1024
RNN_TS = 512
RNN_SEG = RNN_TS // 8
RSQRT_FLOOR = 1e-30
ATT_T = 256
ATT_FAR_UNROLL = 31
ATT_NEAR_UNROLL = 9
V_ROWS = HEAD_W + 16
LOG2E = 1.4426950408889634
FIN_TS = 512

VMEM_LIMIT = 48 * 1024 * 1024
ATT_VMEM_LIMIT = 56 * 1024 * 1024


def _cparams(n_axes):
    return pltpu.CompilerParams(
        dimension_semantics=("arbitrary",) * n_axes,
        vmem_limit_bytes=VMEM_LIMIT,
    )


def _sigmoid(z):
    return 1.0 / (1.0 + jnp.exp(-z))


def _silu(z):
    return z * _sigmoid(z)


def _mod_kernel(c_ref, w_ref, b_ref, o_ref):
    c = c_ref[...]
    o_ref[...] = jnp.dot(_silu(c), w_ref[...], preferred_element_type=F32,
                         precision=lax.Precision.HIGHEST) + b_ref[...]


def _mod_call(c, w_ada, b_ada):
    bsz = c.shape[0]
    n = w_ada.shape[1]
    tn = 1024
    return pl.pallas_call(
        _mod_kernel,
        grid=(n // tn,),
        in_specs=[
            pl.BlockSpec((bsz, D_MODEL), lambda j: (0, 0)),
            pl.BlockSpec((D_MODEL, tn), lambda j: (0, j)),
            pl.BlockSpec((1, tn), lambda j: (0, j)),
        ],
        out_specs=pl.BlockSpec((bsz, tn), lambda j: (0, j)),
        out_shape=jax.ShapeDtypeStruct((bsz, n), F32),
        compiler_params=_cparams(1),
        name="mod",
    )(c, w_ada, b_ada.reshape(1, n))


def _modulated(x_ref, mod_ref):
    shift = mod_ref[:, 0:D_MODEL]
    scale = mod_ref[:, D_MODEL:2 * D_MODEL]
    return (x_ref[...] * (1.0 + scale) + shift).astype(BF16)


def _proj_nat_kernel(x_ref, mod_ref, w_ref, cs_ref, o_ref, *, act):
    u = _modulated(x_ref, mod_ref)
    acc = jnp.dot(u, w_ref[...], preferred_element_type=F32) * cs_ref[...]
    if act == "silu":
        acc = acc * _sigmoid2(acc)
    elif act == "sigmoid":
        acc = _sigmoid2(acc)
    o_ref[...] = acc.astype(o_ref.dtype)


def _proj_nat_call(x, mod3, w, colscale, act):
    bsz, seq, _ = x.shape
    n = w.shape[1]
    out_dtype = BF16
    return pl.pallas_call(
        functools.partial(_proj_nat_kernel, act=act),
        grid=(bsz, seq // PROJ_TS, n // PROJ_TN),
        in_specs=[
            pl.BlockSpec((None, PROJ_TS, D_MODEL), lambda b, s, j: (b, s, 0)),
            pl.BlockSpec((None, 1, 3 * D_MODEL), lambda b, s, j: (b, 0, 0)),
            pl.BlockSpec((D_MODEL, PROJ_TN), lambda b, s, j: (0, j)),
            pl.BlockSpec((1, PROJ_TN), lambda b, s, j: (0, j)),
        ],
        out_specs=pl.BlockSpec((None, PROJ_TS, PROJ_TN), lambda b, s, j: (b, s, j)),
        out_shape=jax.ShapeDtypeStruct((bsz, seq, n), out_dtype),
        compiler_params=_cparams(3),
        name="proj_" + act,
    )(x, mod3, w, colscale)


def _proj_t_kernel(x_ref, mod_ref, wt_ref, o_ref):
    u = _modulated(x_ref, mod_ref)
    acc = lax.dot_general(wt_ref[...], u, (((1,), (1,)), ((), ())),
                          preferred_element_type=F32)
    extra = V_ROWS - HEAD_W
    ones_rows = (lax.broadcasted_iota(jnp.int32, (extra, ATT_T), 0) == 0).astype(o_ref.dtype)
    for jj in range(PROJ_TS // ATT_T):
        for h in range(N_HEADS):
            o_ref[jj, h, 0:HEAD_W, :] = acc[h * HEAD_W:(h + 1) * HEAD_W,
                                            jj * ATT_T:(jj + 1) * ATT_T].astype(o_ref.dtype)
            o_ref[jj, h, HEAD_W:V_ROWS, :] = ones_rows


def _proj_t_call(x, mod3, wt):
    bsz, seq, _ = x.shape
    n = wt.shape[0]
    per = PROJ_TS // ATT_T
    return pl.pallas_call(
        _proj_t_kernel,
        grid=(bsz, seq // PROJ_TS),
        in_specs=[
            pl.BlockSpec((None, PROJ_TS, D_MODEL), lambda b, s: (b, s, 0)),
            pl.BlockSpec((None, 1, 3 * D_MODEL), lambda b, s: (b, 0, 0)),
            pl.BlockSpec((n, D_MODEL), lambda b, s: (0, 0)),
        ],
        out_specs=pl.BlockSpec((None, per, N_HEADS, V_ROWS, ATT_T), lambda b, s: (b, s, 0, 0, 0)),
        out_shape=jax.ShapeDtypeStruct((bsz, seq // ATT_T, N_HEADS, V_ROWS, ATT_T), BF16),
        compiler_params=_cparams(2),
        name="proj_t",
    )(x, mod3, wt)


def _rnn_row_permutation():
    rho = np.arange(RNN_TS)
    pm = np.zeros((RNN_TS, RNN_TS), np.float32)
    pm[rho, (rho % 8) * RNN_SEG + rho // 8] = 1.0
    return jnp.asarray(pm, BF16)


def _sigmoid2(z):
    return 1.0 / (1.0 + jnp.exp2(z * (-LOG2E)))


def _rnn_kernel(first_ref, next_ref, gprev_ref, pm_ref, pmt_ref, cw_ref, cb_ref, wa_ref, ba_ref,
                wx_ref, bx_ref, lam_ref, wo_ref, o_ref, tail, hcar, xp_s, hp_s):
    ts = RNN_TS
    n_groups = ts // 8
    d_model = D_MODEL

    @pl.when(pl.program_id(1) == 0)
    def _():
        tail[...] = jnp.zeros(tail.shape, F32)
        hcar[...] = jnp.zeros((1, d_model), F32)
        hp_s[...] = jnp.zeros(hp_s.shape, hp_s.dtype)
        xp_s[...] = jnp.dot(pm_ref[...], first_ref[...], preferred_element_type=F32)

    sub = lax.broadcasted_iota(jnp.int32, (8, d_model), 0)
    nl = -lam_ref[...]
    softplus = jnp.maximum(nl, 0.0) + jnp.log(1.0 + jnp.exp(-jnp.abs(nl)))
    neg_rate = LRU_C * softplus
    rate2 = neg_rate * (-LOG2E)

    def conv_of(xp, prev):
        last = xp[ts - 8 * (CONV_WIDTH - 1):ts, :]
        lead = []
        for k in range(CONV_WIDTH - 1):
            cur_k = pltpu.roll(last[8 * k:8 * k + 8, :], 1, 0)
            prev_k = pltpu.roll(prev[8 * k:8 * k + 8, :], 1, 0)
            lead.append(jnp.where(sub == 0, prev_k, cur_k))
        xe = jnp.concatenate(lead + [xp], axis=0)
        conv = cb_ref[...] + cw_ref[CONV_WIDTH - 1:CONV_WIDTH, :] * xp
        for d in range(1, CONV_WIDTH):
            r0 = 8 * (CONV_WIDTH - 1 - d)
            conv = conv + cw_ref[CONV_WIDTH - 1 - d:CONV_WIDTH - d, :] * xe[r0:r0 + ts, :]
        return conv, last

    def gates_of(conv, y_prev):
        cb16 = conv.astype(BF16)
        r_parts, i_parts = [], []
        for g in range(GATE_GROUPS):
            cols = slice(g * MXU_TILE, (g + 1) * MXU_TILE)
            xg = cb16[:, cols]
            r_parts.append(jnp.dot(xg, wa_ref[g], preferred_element_type=F32))
            i_parts.append(jnp.dot(xg, wx_ref[g], preferred_element_type=F32))
            o_ref[:, cols] = jnp.dot(y_prev, wo_ref[:, cols],
                                     preferred_element_type=F32).astype(o_ref.dtype)
        return jnp.concatenate(r_parts, axis=1), jnp.concatenate(i_parts, axis=1)

    def scan_of(conv, r_pre, i_pre, state):
        r = 1.0 / (1.0 + jnp.exp2(r_pre + ba_ref[...]))
        ig = 1.0 / (1.0 + jnp.exp2(i_pre + bx_ref[...]))
        a = jnp.exp2(r * rate2)
        one_minus_a2 = jnp.tanh(r * neg_rate) * (1.0 + a * a)
        root = one_minus_a2 * lax.rsqrt(jnp.maximum(one_minus_a2, RSQRT_FLOOR))
        b = root * (ig * conv)
        seg_end = jnp.zeros((8, d_model), F32)
        seg_prod = jnp.ones((8, d_model), F32)
        for g in range(n_groups):
            ag = a[8 * g:8 * g + 8, :]
            seg_end = ag * seg_end + b[8 * g:8 * g + 8, :]
            seg_prod = ag * seg_prod
        seg_in = jnp.zeros((8, d_model), F32)
        for seg in range(8):
            seg_in = jnp.where(sub == seg, state, seg_in)
            state = seg_end[seg:seg + 1, :] + seg_prod[seg:seg + 1, :] * state
        h = seg_in
        hs = []
        for g in range(n_groups):
            h = a[8 * g:8 * g + 8, :] * h + b[8 * g:8 * g + 8, :]
            hs.append(h)
        return jnp.concatenate(hs, axis=0), state

    h_prev = jnp.dot(pmt_ref[...], hp_s[...], preferred_element_type=F32)
    y_prev = (h_prev * gprev_ref[...].astype(F32)).astype(BF16)

    conv, last = conv_of(xp_s[...], tail[...])
    tail[...] = last
    r_pre, i_pre = gates_of(conv, y_prev)

    hp, state = scan_of(conv, r_pre, i_pre, hcar[...])
    hcar[...] = state
    hp_s[...] = hp.astype(BF16)

    xp_s[...] = jnp.dot(pm_ref[...], next_ref[...], preferred_element_type=F32)


def _rnn_call(lin, gates, conv_w, conv_b, wa, ba, wx, bx, lam, w_out):
    bsz, seq, _ = lin.shape
    row = lambda v: v.reshape(1, D_MODEL)
    const2 = lambda b, t: (0, 0)
    pm = _rnn_row_permutation()
    rows = RNN_TS
    n_tiles = seq // rows
    return pl.pallas_call(
        _rnn_kernel,
        grid=(bsz, n_tiles + 1),
        in_specs=[
            pl.BlockSpec((None, rows, D_MODEL), lambda b, t: (b, 0, LIN_RX)),
            pl.BlockSpec((None, rows, D_MODEL),
                         lambda b, t: (b, jnp.minimum(t + 1, n_tiles - 1), LIN_RX)),
            pl.BlockSpec((None, rows, D_MODEL), lambda b, t: (b, jnp.maximum(t - 1, 0), GATE_RG)),
            pl.BlockSpec((RNN_TS, RNN_TS), const2),
            pl.BlockSpec((RNN_TS, RNN_TS), const2),
            pl.BlockSpec((CONV_WIDTH, D_MODEL), const2),
            pl.BlockSpec((1, D_MODEL), const2),
            pl.BlockSpec((GATE_GROUPS, MXU_TILE, MXU_TILE), lambda b, t: (0, 0, 0)),
            pl.BlockSpec((1, D_MODEL), const2),
            pl.BlockSpec((GATE_GROUPS, MXU_TILE, MXU_TILE), lambda b, t: (0, 0, 0)),
            pl.BlockSpec((1, D_MODEL), const2),
            pl.BlockSpec((1, D_MODEL), const2),
            pl.BlockSpec((D_MODEL, D_MODEL), const2),
        ],
        out_specs=pl.BlockSpec((None, rows, D_MODEL), lambda b, t: (b, jnp.maximum(t - 1, 0), 0)),
        out_shape=jax.ShapeDtypeStruct((bsz, seq, D_MODEL), BF16),
        scratch_shapes=[
            pltpu.VMEM((8 * (CONV_WIDTH - 1), D_MODEL), F32),
            pltpu.VMEM((1, D_MODEL), F32),
            pltpu.VMEM((rows, D_MODEL), F32),
            pltpu.VMEM((rows, D_MODEL), BF16),
        ],
        compiler_params=_cparams(2),
        name="rnn",
    )(lin, lin, gates, pm, pm.T, conv_w, row(conv_b), wa, row(ba), wx, row(bx), row(lam), w_out)


def _rel_bias_tile(rb_ref, head, dist):
    n = jnp.maximum(dist, 0)
    nf = jnp.maximum(n, 1).astype(F32)
    large = MAX_EXACT + (jnp.log(nf / MAX_EXACT) / math.log(MAX_DISTANCE / MAX_EXACT)
                         * (N_BUCKETS - MAX_EXACT)).astype(jnp.int32)
    large = jnp.minimum(large, N_BUCKETS - 1)
    bucket = jnp.where(n < MAX_EXACT, n, large)
    far = rb_ref[N_BUCKETS - 1, head]
    out = jnp.zeros(dist.shape, F32)
    for bkt in range(N_BUCKETS - 1):
        out = jnp.where(bucket == bkt, (rb_ref[bkt, head] - far) * LOG2E, out)
    return out


def _attn_tile_tables(nq):
    far_i = [i for i in range(2, nq) for j in range(i - 1)]
    far_j = [j for i in range(2, nq) for j in range(i - 1)]
    near = [(0, 0, 0)] + [t for i in range(1, nq) for t in ((i, i - 1, 1), (i, i, 0))]
    assert len(far_i) % ATT_FAR_UNROLL == 0 and len(near) % ATT_NEAR_UNROLL == 0
    assert ATT_NEAR_UNROLL <= ATT_FAR_UNROLL
    as_i32 = lambda v: jnp.asarray(v, jnp.int32)
    return (as_i32(far_i), as_i32(far_j),
            as_i32([t[0] for t in near]), as_i32([t[1] for t in near]), as_i32([t[2] for t in near]))


def _attn_kernel(far_i, far_j, near_i, near_j, near_kind,
                 rb_ref, lq1_ref, lk1_ref, lq2_ref, lk2_ref, sw_ref, q_ref, k_ref, vt_ref, ag_ref,
                 o_ref, bias2, m_all, acc_all, sbuf, tmbuf):
    t = ATT_T
    nq = q_ref.shape[0] // t
    head = pl.program_id(0)

    @pl.when(pl.program_id(1) == 0)
    def _():
        kk = lax.broadcasted_iota(jnp.int32, (t, t), 0)
        qq = lax.broadcasted_iota(jnp.int32, (t, t), 1)
        dist = qq - kk
        bias2[0] = jnp.where(dist >= 0, _rel_bias_tile(rb_ref, head, dist), -jnp.inf)
        bias2[1] = _rel_bias_tile(rb_ref, head, dist + t)

    def init(i, carry):
        m_all[i] = jnp.full((1, 2 * t), -jnp.inf, F32)
        acc_all[i] = jnp.zeros((V_ROWS, 2 * t), F32)
        return carry

    lax.fori_loop(0, nq, init, 0)

    lane = lax.broadcasted_iota(jnp.int32, (t, HEAD_W), 1)

    def logits(i, j, bias):
        q = q_ref[pl.ds(pl.multiple_of(i * t, t), t), :]
        zero = jnp.zeros_like(q)
        wq = jnp.concatenate([jnp.where(lane < HEAD_DIM, q, zero),
                              jnp.where(lane >= HEAD_DIM, q, zero)], axis=0)
        kj = k_ref[pl.ds(pl.multiple_of(j * t, t), t), :]
        s = lax.dot_general(kj, wq, (((1,), (1,)), ((), ())), preferred_element_type=F32)
        if bias is not None:
            s = s + jnp.concatenate([bias, bias], axis=1)
        return s

    def stash(u, s):
        sbuf[u] = s
        tmbuf[u] = jnp.max(s, axis=0, keepdims=True)

    def consume(u, i, j):
        m_old = m_all[i]
        m_new = jnp.maximum(m_old, tmbuf[u])
        m_all[i] = m_new
        p = jnp.exp2(sbuf[u] - m_new).astype(BF16)
        alpha = jnp.exp2(m_old - m_new)
        pv = jnp.dot(vt_ref[j], p, preferred_element_type=F32)
        acc_all[i] = alpha * acc_all[i] + pv

    far_logits = lambda n: logits(far_i[n], far_j[n], None)
    near_logits = lambda n: logits(near_i[n], near_j[n], bias2[near_kind[n]])

    def steady(tab_i, tab_j, logits_of, unroll):
        def group(g, carry):
            for u in range(unroll):
                n = g * unroll + u
                s_next = logits_of(n + unroll)
                consume(u, tab_i[n], tab_j[n])
                stash(u, s_next)
            return carry

        lax.fori_loop(0, tab_i.shape[0] // unroll - 1, group, 0)

    def drain(tab_i, tab_j, unroll, refill_of, n_refill):
        base = tab_i.shape[0] - unroll
        for u in range(unroll):
            s_next = refill_of(u) if u < n_refill else None
            consume(u, tab_i[base + u], tab_j[base + u])
            if s_next is not None:
                stash(u, s_next)

    for u in range(ATT_FAR_UNROLL):
        stash(u, far_logits(u))
    steady(far_i, far_j, far_logits, ATT_FAR_UNROLL)
    drain(far_i, far_j, ATT_FAR_UNROLL, near_logits, ATT_NEAR_UNROLL)
    steady(near_i, near_j, near_logits, ATT_NEAR_UNROLL)
    drain(near_i, near_j, ATT_NEAR_UNROLL, None, 0)

    lam = (jnp.exp(jnp.sum(lq1_ref[...] * lk1_ref[...], axis=1, keepdims=True))
           - jnp.exp(jnp.sum(lq2_ref[...] * lk2_ref[...], axis=1, keepdims=True))
           + LAMBDA_INIT)

    def finish(i, carry):
        acc = acc_all[i]
        on = acc[0:HEAD_W, :] * (1.0 / acc[HEAD_W:HEAD_W + 1, :])
        o = on[:, :t] - lam * on[:, t:]
        ms = jnp.mean(o * o, axis=0, keepdims=True)
        o = (o * lax.rsqrt(ms + LN_EPS)).T
        o = o * sw_ref[...] * (1.0 - LAMBDA_INIT)
        rows = pl.ds(pl.multiple_of(i * t, t), t)
        o_ref[rows, :] = (o * ag_ref[rows, :].astype(F32)).astype(o_ref.dtype)
        return carry

    lax.fori_loop(0, nq, finish, 0, unroll=4)


def _attn_call(lin, gates, vt, rel_bias, lq1, lk1, lq2, lk2, subln_w):
    bsz, seq, _ = lin.shape
    nt = seq // ATT_T
    hb = D_MODEL // HEAD_W
    vec = lambda v: v.reshape(1, -1)
    const2 = lambda h, b, *_: (0, 0)
    col = lambda c: (lambda h, b, *_: (b, 0, c * hb + h))
    tables = _attn_tile_tables(nt)
    grid_spec = pltpu.PrefetchScalarGridSpec(
        num_scalar_prefetch=len(tables),
        grid=(N_HEADS, bsz),
        in_specs=[
            pl.BlockSpec(memory_space=pltpu.SMEM),
            pl.BlockSpec((1, HEAD_DIM), const2),
            pl.BlockSpec((1, HEAD_DIM), const2),
            pl.BlockSpec((1, HEAD_DIM), const2),
            pl.BlockSpec((1, HEAD_DIM), const2),
            pl.BlockSpec((1, HEAD_W), const2),
            pl.BlockSpec((None, seq, HEAD_W), col(LIN_Q)),
            pl.BlockSpec((None, seq, HEAD_W), col(LIN_K)),
            pl.BlockSpec((None, nt, None, V_ROWS, ATT_T), lambda h, b, *_: (b, 0, h, 0, 0)),
            pl.BlockSpec((None, seq, HEAD_W), col(GATE_AG)),
        ],
        out_specs=pl.BlockSpec((None, seq, HEAD_W), lambda h, b, *_: (b, 0, h)),
        scratch_shapes=[
            pltpu.VMEM((2, ATT_T, ATT_T), F32),
            pltpu.VMEM((nt, 1, 2 * ATT_T), F32),
            pltpu.VMEM((nt, V_ROWS, 2 * ATT_T), F32),
            pltpu.VMEM((max(ATT_FAR_UNROLL, ATT_NEAR_UNROLL), ATT_T, 2 * ATT_T), F32),
            pltpu.VMEM((max(ATT_FAR_UNROLL, ATT_NEAR_UNROLL), 1, 2 * ATT_T), F32),
        ],
    )
    return pl.pallas_call(
        _attn_kernel,
        grid_spec=grid_spec,
        out_shape=jax.ShapeDtypeStruct((bsz, seq, D_MODEL), BF16),
        compiler_params=pltpu.CompilerParams(
            dimension_semantics=("arbitrary", "arbitrary"),
            vmem_limit_bytes=ATT_VMEM_LIMIT,
        ),
        name="attn",
    )(*tables, rel_bias, vec(lq1), vec(lk1), vec(lq2), vec(lk2), vec(subln_w), lin, lin, vt, gates)


def _final_kernel(x_ref, mod_ref, yr_ref, ya_ref, mg_ref, wa_ref, wo_ref, g_ref, b_ref, o_ref):
    gate = mod_ref[:, 2 * D_MODEL:3 * D_MODEL]
    y_attn = jnp.dot(ya_ref[...], wa_ref[...], preferred_element_type=F32)
    mg = mg_ref[...].astype(F32)
    merged = mg[:, :D_MODEL] * yr_ref[...].astype(F32) + mg[:, D_MODEL:] * y_attn
    out = jnp.dot(merged.astype(BF16), wo_ref[...], preferred_element_type=F32) * gate
    z = DEEPNORM_ALPHA * x_ref[...] + out
    mu = jnp.mean(z, axis=-1, keepdims=True)
    zc = z - mu
    var = jnp.mean(zc * zc, axis=-1, keepdims=True)
    o_ref[...] = zc * lax.rsqrt(var + LN_EPS) * g_ref[...] + b_ref[...]


def _final_call(x, mod3, y_rnn, y_attn_pre, merge, w_attn_out, w_out, ln_g, ln_b):
    bsz, seq, _ = x.shape
    row = lambda v: v.reshape(1, D_MODEL)
    const2 = lambda b, s: (0, 0)
    tile = lambda: pl.BlockSpec((None, FIN_TS, D_MODEL), lambda b, s: (b, s, 0))
    return pl.pallas_call(
        _final_kernel,
        grid=(bsz, seq // FIN_TS),
        in_specs=[
            tile(),
            pl.BlockSpec((None, 1, 3 * D_MODEL), lambda b, s: (b, 0, 0)),
            tile(),
            tile(),
            pl.BlockSpec((None, FIN_TS, 2 * D_MODEL), lambda b, s: (b, s, 0)),
            pl.BlockSpec((D_MODEL, D_MODEL), const2),
            pl.BlockSpec((D_MODEL, D_MODEL), const2),
            pl.BlockSpec((1, D_MODEL), const2),
            pl.BlockSpec((1, D_MODEL), const2),
        ],
        out_specs=tile(),
        out_shape=jax.ShapeDtypeStruct((bsz, seq, D_MODEL), F32),
        compiler_params=_cparams(2),
        name="final",
    )(x, mod3, y_rnn, y_attn_pre, merge, w_attn_out, w_out, row(ln_g), row(ln_b))


def _block_diag_groups(w):
    per = MXU_TILE // RNN_BLOCK
    w = w.reshape(GATE_GROUPS, per, RNN_BLOCK, RNN_BLOCK)
    eye = jnp.eye(per, dtype=w.dtype)
    return jnp.einsum("gaij,ab->gaibj", w, eye).reshape(GATE_GROUPS, MXU_TILE, MXU_TILE)


def kernel(x, c, w_ada, b_ada, w_in, conv_w, conv_b, w_rg_a, b_rg_a, w_rg_x, b_rg_x, lru_lambda,
           lambda_q1, lambda_k1, lambda_q2, lambda_k2, subln_w, rel_bias, w_rnn_out, w_attn_out,
           w_out, ln_g, ln_b):
    layer = 0
    bsz = x.shape[0]
    d = D_MODEL

    mod3 = _mod_call(c, w_ada[layer], b_ada[layer]).reshape(bsz, 1, 3 * d)

    w_in16 = w_in[layer].astype(BF16)
    w_lin = jnp.concatenate([w_in16[:, 0:d], w_in16[:, 2 * d:4 * d]], axis=1)
    w_gate = jnp.concatenate([w_in16[:, d:2 * d], w_in16[:, 5 * d:6 * d]], axis=1)
    w_merge = w_in16[:, 6 * d:8 * d]
    w_vt = w_in16[:, 4 * d:5 * d].T
    ones = jnp.ones((1, d), F32)
    q_scale = jnp.full((1, d), LOG2E * HEAD_DIM ** -0.5, F32)

    lin = _proj_nat_call(x, mod3, w_lin, jnp.concatenate([ones, q_scale, ones], axis=1), "none")
    gates = _proj_nat_call(x, mod3, w_gate, jnp.concatenate([ones, ones], axis=1), "silu")
    merge = _proj_nat_call(x, mod3, w_merge, jnp.concatenate([ones, ones], axis=1), "sigmoid")
    vt = _proj_t_call(x, mod3, w_vt)

    y_rnn = _rnn_call(lin, gates, conv_w[layer], conv_b[layer],
                      _block_diag_groups(w_rg_a[layer] * (-LOG2E)).astype(BF16),
                      b_rg_a[layer] * (-LOG2E),
                      _block_diag_groups(w_rg_x[layer] * (-LOG2E)).astype(BF16),
                      b_rg_x[layer] * (-LOG2E),
                      lru_lambda[layer], w_rnn_out[layer].astype(BF16))

    y_attn_pre = _attn_call(lin, gates, vt, rel_bias, lambda_q1[layer], lambda_k1[layer],
                            lambda_q2[layer], lambda_k2[layer], subln_w[layer])

    return _final_call(x, mod3, y_rnn, y_attn_pre, merge, w_attn_out[layer].astype(BF16),
                       w_out[layer].astype(BF16), ln_g[layer], ln_b[layer])
```
